```python
import jax, jax.numpy as jnp
from jax import lax
import numpy as np


D_MODEL = 1024
BATCH = 4
SEQ = 4096
DEPTH = 1
DEC_BATCH = 2
DEC_SEQ = 16384
PAST_LEN = 128

HEAD_DIM = 64
HEADS_PER_GROUP = 8
DILATION_GROUPS = ((128, 1), (512, 4), (2048, 16))
N_GROUPS = 3
N_ATT_HEADS = N_GROUPS * HEADS_PER_GROUP
ATT_WIDTH = N_ATT_HEADS * HEAD_DIM
ATT_OUT_WIDTH = HEADS_PER_GROUP * HEAD_DIM
CONV_DIM = D_MODEL
CONV_WIDTH = 3
D_FF = 4 * D_MODEL
ROPE_THETA = 10000.0
EPS = 1e-6
NEG = -1e30
SPLIT_SIZES = (ATT_WIDTH, ATT_WIDTH, ATT_WIDTH, CONV_DIM, CONV_DIM, CONV_DIM, D_MODEL, D_MODEL)
IN_WIDTH = 3 * ATT_WIDTH + 3 * CONV_DIM + 2 * D_MODEL

kernel_name = "hybrid_dilated_attn_shortconv_encoder"


def rms_norm(x, g):
    xf = x.astype(jnp.float32)
    y = xf * lax.rsqrt(jnp.mean(xf * xf, axis=-1, keepdims=True) + EPS)
    return (y * g.astype(jnp.float32)).astype(x.dtype)


def rope(x, pos):
    half = HEAD_DIM // 2
    inv = 1.0 / (ROPE_THETA ** (jnp.arange(half, dtype=jnp.float32) / half))
    ang = pos.astype(jnp.float32)[:, None] * inv[None, :]
    cos = jnp.cos(ang)[None, :, None, :]
    sin = jnp.sin(ang)[None, :, None, :]
    xf = x.astype(jnp.float32)
    x1, x2 = xf[..., :half], xf[..., half:]
    return jnp.concatenate([x1 * cos - x2 * sin, x2 * cos + x1 * sin], axis=-1).astype(x.dtype)


def banded_attention(q, k, v, half):
    n, L, H, hd = q.shape
    blk = half
    nb = -(-L // blk)
    pad = nb * blk - L
    qb = jnp.pad(q, ((0, 0), (0, pad), (0, 0), (0, 0))).reshape(n, nb, blk, H, hd)
    kp = jnp.pad(k, ((0, 0), (blk, pad + blk), (0, 0), (0, 0))).reshape(n, nb + 2, blk, H, hd)
    vp = jnp.pad(v, ((0, 0), (blk, pad + blk), (0, 0), (0, 0))).reshape(n, nb + 2, blk, H, hd)
    kb = jnp.concatenate([kp[:, :-2], kp[:, 1:-1], kp[:, 2:]], axis=2)
    vb = jnp.concatenate([vp[:, :-2], vp[:, 1:-1], vp[:, 2:]], axis=2)
    s = jnp.einsum("nbqhd,nbkhd->nbhqk", qb, kb, preferred_element_type=jnp.float32) * (hd ** -0.5)
    qpos = jnp.arange(nb)[:, None] * blk + jnp.arange(blk)[None, :]
    kpos = jnp.arange(nb)[:, None] * blk - blk + jnp.arange(3 * blk)[None, :]
    rel = kpos[:, None, :] - qpos[:, :, None]
    valid = (jnp.abs(rel) <= half) & (kpos[:, None, :] >= 0) & (kpos[:, None, :] < L)
    s = jnp.where(valid[None, :, None, :, :], s, NEG)
    m = jnp.max(s, axis=-1, keepdims=True)
    p = jnp.exp(s - m)
    den = jnp.sum(p, axis=-1)
    o = jnp.einsum("nbhqk,nbkhd->nbqhd", p.astype(v.dtype), vb, preferred_element_type=jnp.float32)
    den_t = jnp.transpose(den, (0, 1, 3, 2))
    o = o / den_t[..., None]
    lse = jnp.transpose(m[..., 0], (0, 1, 3, 2)) + jnp.log(den_t)
    o = o.reshape(n, nb * blk, H, hd)[:, :L]
    lse = lse.reshape(n, nb * blk, H)[:, :L]
    return o, lse


def dilated_group(q, k, v, window, dil):
    B, S, H, hd = q.shape
    L = S // dil
    half = window // (2 * dil)

    def to_res(t):
        return t.reshape(B, L, dil, H, hd).transpose(0, 2, 1, 3, 4).reshape(B * dil, L, H, hd)

    o, lse = banded_attention(to_res(q), to_res(k), to_res(v), half)
    o = o.reshape(B, dil, L, H, hd).transpose(0, 2, 1, 3, 4).reshape(B, S, H, hd)
    lse = lse.reshape(B, dil, L, H).transpose(0, 2, 1, 3).reshape(B, S, H)
    return o, lse


def token_mixer(xn, w_in, conv_w, conv_b, w_attn_out, w_conv_out, w_mix_out):
    B, S, _ = xn.shape
    proj = jnp.einsum("bsd,de->bse", xn, w_in)
    points = []
    acc = 0
    for sz in SPLIT_SIZES[:-1]:
        acc += sz
        points.append(acc)
    q, k, v, h, c_gate, b_gate, g_att, g_conv = jnp.split(proj, points, axis=-1)
    pos = jnp.arange(S)
    q = rope(q.reshape(B, S, N_ATT_HEADS, HEAD_DIM), pos)
    k = rope(k.reshape(B, S, N_ATT_HEADS, HEAD_DIM), pos)
    v = v.reshape(B, S, N_ATT_HEADS, HEAD_DIM)
    outs = []
    lses = []
    for gi, (window, dil) in enumerate(DILATION_GROUPS):
        lo, hi = gi * HEADS_PER_GROUP, (gi + 1) * HEADS_PER_GROUP
        o, lse = dilated_group(q[:, :, lo:hi], k[:, :, lo:hi], v[:, :, lo:hi], window, dil)
        outs.append(o)
        lses.append(lse)
    wts = jax.nn.softmax(jnp.stack(lses, axis=0), axis=0)
    att = jnp.sum(wts[..., None] * jnp.stack(outs, axis=0), axis=0)
    att = att.astype(xn.dtype).reshape(B, S, ATT_OUT_WIDTH)
    att = jnp.einsum("bse,ed->bsd", att, w_attn_out)
    u = c_gate * h
    r = CONV_WIDTH // 2
    up = jnp.pad(u, ((0, 0), (r, r), (0, 0)))
    conv = conv_b[None, None, :]
    for t in range(CONV_WIDTH):
        conv = conv + up[:, t:t + S] * conv_w[t][None, None, :]
    cv = jnp.einsum("bse,ed->bsd", b_gate * conv, w_conv_out)
    merged = jax.nn.sigmoid(g_att) * att + jax.nn.sigmoid(g_conv) * cv
    return jnp.einsum("bsd,de->bse", merged, w_mix_out)


def encoder(x, c, w_ada, b_ada, norm1_g, w_in, conv_w, conv_b, w_attn_out, w_conv_out,
            w_mix_out, norm2_g, w_mlp_in, w_mlp_out, final_norm_g):
    for l in range(DEPTH):
        mod = jnp.einsum("bd,de->be", jax.nn.silu(c), w_ada[l]) + b_ada[l]
        sh1, sc1, gt1, sh2, sc2, gt2 = jnp.split(mod, 6, axis=-1)
        xn = rms_norm(x, norm1_g[l]) * (1.0 + sc1[:, None, :]) + sh1[:, None, :]
        x = x + gt1[:, None, :] * token_mixer(xn, w_in[l], conv_w[l], conv_b[l], w_attn_out[l],
                                              w_conv_out[l], w_mix_out[l])
        xn = rms_norm(x, norm2_g[l]) * (1.0 + sc2[:, None, :]) + sh2[:, None, :]
        hdn = jnp.square(jax.nn.relu(jnp.einsum("bsd,df->bsf", xn, w_mlp_in[l])))
        x = x + gt2[:, None, :] * jnp.einsum("bsf,fd->bsd", hdn, w_mlp_out[l])
    return rms_norm(x, final_norm_g)


def setup_inputs(seed: int = 0) -> dict:
    key = jax.random.key(seed)
    ks = jax.random.split(key, 20)
    f32 = jnp.float32
    nrm = lambda k, shape, s: jax.random.normal(k, shape, f32) * s
    return {
        "x_prompt": nrm(ks[0], (BATCH, SEQ, D_MODEL), 1.0),
        "x_sample": nrm(ks[1], (DEC_BATCH, DEC_SEQ, D_MODEL), 1.0),
        "c_prompt": nrm(ks[2], (BATCH, D_MODEL), 1.0),
        "c_sample": nrm(ks[3], (DEC_BATCH, D_MODEL), 1.0),
        "w_ada": nrm(ks[4], (DEPTH, D_MODEL, 6 * D_MODEL), 0.5 * D_MODEL ** -0.5),
        "b_ada": nrm(ks[5], (DEPTH, 6 * D_MODEL), 0.02),
        "norm1_g": 1.0 + nrm(ks[6], (DEPTH, D_MODEL), 0.02),
        "w_in": nrm(ks[7], (DEPTH, D_MODEL, IN_WIDTH), D_MODEL ** -0.5),
        "conv_w": nrm(ks[8], (DEPTH, CONV_WIDTH, CONV_DIM), CONV_WIDTH ** -0.5),
        "conv_b": nrm(ks[9], (DEPTH, CONV_DIM), 0.02),
        "w_attn_out": nrm(ks[10], (DEPTH, ATT_OUT_WIDTH, D_MODEL), ATT_OUT_WIDTH ** -0.5),
        "w_conv_out": nrm(ks[11], (DEPTH, CONV_DIM, D_MODEL), CONV_DIM ** -0.5),
        "w_mix_out": nrm(ks[12], (DEPTH, D_MODEL, D_MODEL), D_MODEL ** -0.5),
        "norm2_g": 1.0 + nrm(ks[13], (DEPTH, D_MODEL), 0.02),
        "w_mlp_in": nrm(ks[14], (DEPTH, D_MODEL, D_FF), D_MODEL ** -0.5),
        "w_mlp_out": nrm(ks[15], (DEPTH, D_FF, D_MODEL), D_FF ** -0.5),
        "final_norm_g": 1.0 + nrm(ks[16], (D_MODEL,), 0.02),
    }


def reference(x_prompt, x_sample, c_prompt, c_sample, w_ada, b_ada, norm1_g, w_in, conv_w, conv_b,
              w_attn_out, w_conv_out, w_mix_out, norm2_g, w_mlp_in, w_mlp_out, final_norm_g):
    y_prompt = encoder(x_prompt, c_prompt, w_ada, b_ada, norm1_g, w_in, conv_w, conv_b, w_attn_out,
                       w_conv_out, w_mix_out, norm2_g, w_mlp_in, w_mlp_out, final_norm_g)
    y_sample = encoder(x_sample, c_sample, w_ada, b_ada, norm1_g, w_in, conv_w, conv_b, w_attn_out,
                       w_conv_out, w_mix_out, norm2_g, w_mlp_in, w_mlp_out, final_norm_g)
    return (y_prompt, y_sample)
```

```python
import functools

import jax
import jax.numpy as jnp
from jax import lax
from jax.experimental import pallas as pl
from jax.experimental.pallas import tpu as pltpu

D_MODEL = 1024
HEAD_DIM = 64
HEADS_PER_GROUP = 8
GROUP_WIDTH = HEADS_PER_GROUP * HEAD_DIM
DILATION_GROUPS = ((128, 1), (512, 4), (2048, 16))
DILS = tuple(d for _, d in DILATION_GROUPS)
BAND = DILATION_GROUPS[0][0] // (2 * DILATION_GROUPS[0][1])
assert all(w // (2 * d) == BAND for w, d in DILATION_GROUPS)
N_GROUPS = len(DILS)
ATT_WIDTH = N_GROUPS * GROUP_WIDTH
CONV_WIDTH = 3
D_FF = 4 * D_MODEL
ROPE_THETA = 10000.0
EPS = 1e-6
NEG = -1e30

LANES = 128
HALF_ROT = HEAD_DIM // 2

QKV_TILE = 512
ATT_TILE = 1024
VMEM_LIMIT = 60 * 1024 * 1024

F32 = jnp.float32
BF16 = jnp.bfloat16


def _resident(shape):
    return pl.BlockSpec(shape, lambda *_: (0,) * len(shape), pipeline_mode=pl.Buffered(1))


def _params(n_axes):
    return pltpu.CompilerParams(
        dimension_semantics=("parallel",) * n_axes, vmem_limit_bytes=VMEM_LIMIT)


def _modulated_norm(x, gain, scale, shift):
    r = lax.rsqrt(jnp.mean(x * x, axis=-1, keepdims=True) + EPS)
    return (x * r) * (gain * (1.0 + scale)) + shift


def _ada_kernel(c_ref, w_ref, b_ref, o_ref):
    c = c_ref[...]
    s = c * jax.nn.sigmoid(c)
    o_ref[...] = jnp.dot(s, w_ref[...], precision=lax.Precision.HIGHEST,
                         preferred_element_type=F32) + b_ref[...]


def _ada(c_all, w_ada, b_ada):
    rows, _ = c_all.shape
    n_out = w_ada.shape[1]
    bn = n_out // 4
    return pl.pallas_call(
        _ada_kernel,
        grid=(n_out // bn,),
        in_specs=[
            pl.BlockSpec((rows, D_MODEL), lambda j: (0, 0)),
            pl.BlockSpec((D_MODEL, bn), lambda j: (0, j)),
            pl.BlockSpec((1, bn), lambda j: (0, j)),
        ],
        out_specs=pl.BlockSpec((rows, bn), lambda j: (0, j)),
        out_shape=jax.ShapeDtypeStruct((rows, n_out), F32),
        compiler_params=_params(1),
        name="ada",
    )(c_all, w_ada, b_ada)


def _qkv_kernel(x_ref, mod_ref, g1_ref, cos_ref, sin_ref, w_ref, *rest):
    qkv_refs = rest[:3 * N_GROUPS]
    u_ref = rest[3 * N_GROUPS]
    scr_ref = rest[3 * N_GROUPS + 1]
    tm = x_ref.shape[1]

    mod = mod_ref[0]
    xn = _modulated_norm(x_ref[0], g1_ref[...], mod[1:2], mod[0:1])
    xb = xn.astype(BF16)
    cos = cos_ref[...]
    sin = sin_ref[...]

    def emit(out_ref, dil, val):
        if dil == 1:
            out_ref[0, 0] = val.astype(BF16)
        else:
            for p in range(GROUP_WIDTH // LANES):
                scr_ref[p] = val[:, p * LANES:(p + 1) * LANES]
            for r in range(dil):
                for p in range(GROUP_WIDTH // LANES):
                    out_ref[0, r, :, p * LANES:(p + 1) * LANES] = (
                        scr_ref[p, pl.ds(r, tm // dil, stride=dil), :].astype(BF16))

    def rope(val):
        parts = []
        for p in range(GROUP_WIDTH // LANES):
            xp = val[:, p * LANES:(p + 1) * LANES]
            parts.append(xp * cos + pltpu.roll(xp, 2 * HALF_ROT, 1) * sin)
        return jnp.concatenate(parts, axis=1)

    for which in range(3):
        for g, dil in enumerate(DILS):
            col = which * ATT_WIDTH + g * GROUP_WIDTH
            val = jnp.dot(xb, w_ref[:, col:col + GROUP_WIDTH], preferred_element_type=F32)
            if which < 2:
                val = rope(val)
            emit(qkv_refs[3 * g + which], dil, val)

    col = 3 * ATT_WIDTH
    h = jnp.dot(xb, w_ref[:, col:col + D_MODEL], preferred_element_type=F32)
    cg = jnp.dot(xb, w_ref[:, col + D_MODEL:col + 2 * D_MODEL], preferred_element_type=F32)
    u_ref[0] = cg * h


def _qkv(x, mod, g1, cos_t, sin_t, w_qkvhc):
    nb, seq, _ = x.shape
    tm = QKV_TILE
    n_cols = w_qkvhc.shape[1]
    out_shapes = []
    out_specs = []
    for dil in DILS:
        for _ in range(3):
            out_shapes.append(jax.ShapeDtypeStruct((nb, dil, seq // dil, GROUP_WIDTH), BF16))
            out_specs.append(pl.BlockSpec((1, dil, tm // dil, GROUP_WIDTH), lambda b, i: (b, 0, i, 0)))
    out_shapes.append(jax.ShapeDtypeStruct((nb, seq, D_MODEL), F32))
    out_specs.append(pl.BlockSpec((1, tm, D_MODEL), lambda b, i: (b, i, 0)))
    return pl.pallas_call(
        _qkv_kernel,
        grid=(nb, seq // tm),
        in_specs=[
            pl.BlockSpec((1, tm, D_MODEL), lambda b, i: (b, i, 0)),
            pl.BlockSpec((1, 6, D_MODEL), lambda b, i: (b, 0, 0)),
            pl.BlockSpec((1, D_MODEL), lambda b, i: (0, 0)),
            pl.BlockSpec((tm, LANES), lambda b, i: (i, 0)),
            pl.BlockSpec((tm, LANES), lambda b, i: (i, 0)),
            _resident((D_MODEL, n_cols)),
        ],
        out_specs=out_specs,
        out_shape=out_shapes,
        scratch_shapes=[pltpu.VMEM((GROUP_WIDTH // LANES, tm, LANES), F32)],
        compiler_params=_params(2),
        name="qkv",
    )(x, mod, g1, cos_t, sin_t, w_qkvhc)


def _attn_kernel(*refs):
    in_refs = refs[:7 * N_GROUPS]
    att_ref = refs[7 * N_GROUPS]
    o_scr, lse_scr = refs[7 * N_GROUPS + 1:7 * N_GROUPS + 3]
    kv_scr = refs[7 * N_GROUPS + 3:]
    tm = att_ref.shape[1]
    i = pl.program_id(1)
    last = pl.num_programs(1) - 1

    n_pairs = GROUP_WIDTH // LANES
    lane = lax.broadcasted_iota(jnp.int32, (1, LANES), 1)
    qk_head_a = (lane // HALF_ROT) % 2 == 0
    v_head_a = lane < HEAD_DIM

    for g, dil in enumerate(DILS):
        q_ref, kp_ref, kc_ref, kn_ref, vp_ref, vc_ref, vn_ref = in_refs[7 * g:7 * g + 7]
        kf_ref, vf_ref = kv_scr[2 * g:2 * g + 2]
        n = tm // dil
        tq = min(2 * BAND, n)
        tk = tq + 2 * BAND
        nsub = n // tq

        row = lax.broadcasted_iota(jnp.int32, (tq, tk), 0)
        colk = lax.broadcasted_iota(jnp.int32, (tq, tk), 1)
        rel = colk - BAND - row
        band = (rel >= -BAND) & (rel <= BAND)

        def residue_body(r, carry, q_ref=q_ref, kp_ref=kp_ref, kc_ref=kc_ref, kn_ref=kn_ref,
                         vp_ref=vp_ref, vc_ref=vc_ref, vn_ref=vn_ref, kf_ref=kf_ref, vf_ref=vf_ref,
                         n=n, tq=tq, tk=tk, nsub=nsub, dil=dil, g=g, band=band, colk=colk):
            kf_ref[0:BAND] = kp_ref[0, r]
            kf_ref[BAND:BAND + n] = kc_ref[0, r]
            kf_ref[BAND + n:] = kn_ref[0, r]
            vf_ref[0:BAND] = vp_ref[0, r]
            vf_ref[BAND:BAND + n] = vc_ref[0, r]
            vf_ref[BAND + n:] = vn_ref[0, r]

            def sub_body(sub, carry2):
                q0 = pl.multiple_of(sub * tq, tq)
                lo = jnp.where((i == 0) & (sub == 0), BAND, 0)
                hi = jnp.where((i == last) & (sub == nsub - 1), tk - BAND, tk)
                valid = band & (colk >= lo) & (colk < hi)
                bias = jnp.where(valid, 0.0, NEG).astype(F32)
                for p in range(GROUP_WIDTH // LANES):
                    cols = slice(p * LANES, (p + 1) * LANES)
                    qp = q_ref[0, r, pl.ds(q0, tq), cols]
                    kp = kf_ref[pl.ds(q0, tk), cols]
                    vp = vf_ref[pl.ds(q0, tk), cols]
                    outs = []
                    for head_a in (True, False):
                        mask = qk_head_a if head_a else jnp.logical_not(qk_head_a)
                        qh = jnp.where(mask, qp, jnp.zeros_like(qp))
                        s = lax.dot_general(qh, kp, (((1,), (1,)), ((), ())),
                                            preferred_element_type=F32)
                        s = s + bias
                        m = jnp.max(s, axis=-1, keepdims=True)
                        e = jnp.exp(s - m)
                        den = jnp.sum(e, axis=-1, keepdims=True)
                        o = jnp.dot(e.astype(BF16), vp, preferred_element_type=F32)
                        outs.append((o, m, den))
                    (oa, ma, da), (ob, mb, db) = outs
                    o2 = jnp.where(v_head_a, oa, ob)
                    m2 = jnp.where(v_head_a, ma, mb)
                    d2 = jnp.where(v_head_a, da, db)
                    o_n = o2 / d2
                    lse = m2 + jnp.log(d2)
                    if dil == 1:
                        rows = pl.ds(q0, tq)
                    else:
                        rows = pl.ds(r + dil * q0, tq, stride=dil)
                    o_scr[g * n_pairs + p, rows, :] = o_n
                    lse_scr[g * n_pairs + p, rows, :] = lse
                return carry2

            lax.fori_loop(0, nsub, sub_body, 0)
            return carry

        lax.fori_loop(0, dil, residue_body, 0)

    chunk = 128

    def merge_body(c, carry):
        rows = pl.ds(pl.multiple_of(c * chunk, chunk), chunk)
        for p in range(n_pairs):
            ls = [lse_scr[g * n_pairs + p, rows, :] for g in range(N_GROUPS)]
            mx = functools.reduce(jnp.maximum, ls)
            ws = [jnp.exp(l - mx) for l in ls]
            num = sum(w * o_scr[g * n_pairs + p, rows, :] for g, w in enumerate(ws))
            att_ref[0, rows, p * LANES:(p + 1) * LANES] = (num / sum(ws)).astype(att_ref.dtype)
        return carry

    lax.fori_loop(0, tm // chunk, merge_body, 0)


def _attn(qkv):
    nb = qkv[0].shape[0]
    seq = qkv[0].shape[1] * qkv[0].shape[2]
    tm = ATT_TILE
    operands = []
    in_specs = []
    merge_shape = (N_GROUPS * GROUP_WIDTH // LANES, tm, LANES)
    scratch = [pltpu.VMEM(merge_shape, F32), pltpu.VMEM(merge_shape, F32)]
    for g, dil in enumerate(DILS):
        q, k, v = qkv[3 * g:3 * g + 3]
        n = tm // dil
        nblk = n // BAND
        n_halo = seq // dil // BAND
        cur = pl.BlockSpec((1, dil, n, GROUP_WIDTH), lambda b, i: (b, 0, i, 0))
        prev = pl.BlockSpec((1, dil, BAND, GROUP_WIDTH),
                            lambda b, i, nblk=nblk: (b, 0, jnp.maximum(i * nblk - 1, 0), 0))
        nxt = pl.BlockSpec((1, dil, BAND, GROUP_WIDTH),
                           lambda b, i, nblk=nblk, n_halo=n_halo: (b, 0, jnp.minimum((i + 1) * nblk, n_halo - 1), 0))
        operands += [q, k, k, k, v, v, v]
        in_specs += [cur, prev, cur, nxt, prev, cur, nxt]
        scratch += [pltpu.VMEM((n + 2 * BAND, GROUP_WIDTH), BF16)] * 2
    return pl.pallas_call(
        _attn_kernel,
        grid=(nb, seq // tm),
        in_specs=in_specs,
        out_specs=pl.BlockSpec((1, tm, GROUP_WIDTH), lambda b, i: (b, i, 0)),
        out_shape=jax.ShapeDtypeStruct((nb, seq, GROUP_WIDTH), BF16),
        scratch_shapes=scratch,
        compiler_params=_params(2),
        name="attn",
    )(*operands)


def _mix_kernel(x_ref, att_ref, u_ref, up_ref, un_ref, mod_ref, g1_ref, cw_ref, cb_ref,
                w3_ref, wa_ref, wc_ref, wm_ref, o_ref):
    tm = x_ref.shape[1]
    i = pl.program_id(1)
    last = pl.num_programs(1) - 1
    mod = mod_ref[0]
    x = x_ref[0]
    xb = _modulated_norm(x, g1_ref[...], mod[1:2], mod[0:1]).astype(BF16)

    u = u_ref[0]
    halo = up_ref.shape[1]
    u_before = jnp.where(i > 0, up_ref[0, halo - 1:halo, :], 0.0)
    u_after = jnp.where(i < last, un_ref[0, 0:1, :], 0.0)
    rows = lax.broadcasted_iota(jnp.int32, (tm, 1), 0)
    u_m1 = jnp.where(rows == 0, u_before, pltpu.roll(u, 1, 0))
    u_p1 = jnp.where(rows == tm - 1, u_after, pltpu.roll(u, tm - 1, 0))
    cw = cw_ref[...]
    conv = cb_ref[...] + u_m1 * cw[0:1] + u * cw[1:2] + u_p1 * cw[2:3]

    b_gate = jnp.dot(xb, w3_ref[:, 0:D_MODEL], preferred_element_type=F32)
    cv = jnp.dot((b_gate * conv).astype(BF16), wc_ref[...], preferred_element_type=F32)
    g_att = jnp.dot(xb, w3_ref[:, D_MODEL:2 * D_MODEL], preferred_element_type=F32)
    ap = jnp.dot(att_ref[0], wa_ref[...], preferred_element_type=F32)
    merged = jax.nn.sigmoid(g_att) * ap
    g_conv = jnp.dot(xb, w3_ref[:, 2 * D_MODEL:3 * D_MODEL], preferred_element_type=F32)
    merged = merged + jax.nn.sigmoid(g_conv) * cv
    mix = jnp.dot(merged.astype(BF16), wm_ref[...], preferred_element_type=F32)
    o_ref[0] = x + mod[2:3] * mix


def _mix(x, att, u, mod, g1, conv_w, conv_b, w_bgg, w_att, w_conv, w_mixo):
    nb, seq, _ = x.shape
    tm = QKV_TILE
    halo = 8
    nh = tm // halo
    return pl.pallas_call(
        _mix_kernel,
        grid=(nb, seq // tm),
        in_specs=[
            pl.BlockSpec((1, tm, D_MODEL), lambda b, i: (b, i, 0)),
            pl.BlockSpec((1, tm, GROUP_WIDTH), lambda b, i: (b, i, 0)),
            pl.BlockSpec((1, tm, D_MODEL), lambda b, i: (b, i, 0)),
            pl.BlockSpec((1, halo, D_MODEL), lambda b, i: (b, jnp.maximum(i * nh - 1, 0), 0)),
            pl.BlockSpec((1, halo, D_MODEL), lambda b, i: (b, jnp.minimum((i + 1) * nh, seq // halo - 1), 0)),
            pl.BlockSpec((1, 6, D_MODEL), lambda b, i: (b, 0, 0)),
            pl.BlockSpec((1, D_MODEL), lambda b, i: (0, 0)),
            pl.BlockSpec((CONV_WIDTH, D_MODEL), lambda b, i: (0, 0)),
            pl.BlockSpec((1, D_MODEL), lambda b, i: (0, 0)),
            _resident(w_bgg.shape),
            _resident(w_att.shape),
            _resident(w_conv.shape),
            _resident(w_mixo.shape),
        ],
        out_specs=pl.BlockSpec((1, tm, D_MODEL), lambda b, i: (b, i, 0)),
        out_shape=jax.ShapeDtypeStruct((nb, seq, D_MODEL), F32),
        compiler_params=_params(2),
        name="mix",
    )(x, att, u, u, u, mod, g1, conv_w, conv_b, w_bgg, w_att, w_conv, w_mixo)


def _mlp_kernel(x_ref, mod_ref, g2_ref, gf_ref, w1_ref, w2_ref, o_ref):
    mod = mod_ref[0]
    x = x_ref[0]
    xb = _modulated_norm(x, g2_ref[...], mod[4:5], mod[3:4]).astype(BF16)
    ff_chunk = D_MODEL
    acc = jnp.zeros(x.shape, F32)
    for c in range(D_FF // ff_chunk):
        cols = slice(c * ff_chunk, (c + 1) * ff_chunk)
        h = jnp.dot(xb, w1_ref[:, cols], preferred_element_type=F32)
        h = jnp.square(jnp.maximum(h, 0.0)).astype(BF16)
        acc = acc + jnp.dot(h, w2_ref[cols, :], preferred_element_type=F32)
    x2 = x + mod[5:6] * acc
    r = lax.rsqrt(jnp.mean(x2 * x2, axis=-1, keepdims=True) + EPS)
    o_ref[0] = (x2 * r) * gf_ref[...]


def _mlp(x1, mod, g2, gf, w1, w2):
    nb, seq, _ = x1.shape
    tm = QKV_TILE
    return pl.pallas_call(
        _mlp_kernel,
        grid=(nb, seq // tm),
        in_specs=[
            pl.BlockSpec((1, tm, D_MODEL), lambda b, i: (b, i, 0)),
            pl.BlockSpec((1, 6, D_MODEL), lambda b, i: (b, 0, 0)),
            pl.BlockSpec((1, D_MODEL), lambda b, i: (0, 0)),
            pl.BlockSpec((1, D_MODEL), lambda b, i: (0, 0)),
            _resident(w1.shape),
            _resident(w2.shape),
        ],
        out_specs=pl.BlockSpec((1, tm, D_MODEL), lambda b, i: (b, i, 0)),
        out_shape=jax.ShapeDtypeStruct((nb, seq, D_MODEL), F32),
        compiler_params=_params(2),
        name="mlp",
    )(x1, mod, g2, gf, w1, w2)


def _rope_tables(seq):
    inv = 1.0 / (ROPE_THETA ** (jnp.arange(HALF_ROT, dtype=F32) / HALF_ROT))
    ang = jnp.arange(seq).astype(F32)[:, None] * inv[None, :]
    cos = jnp.cos(ang)
    sin = jnp.sin(ang)
    return jnp.tile(cos, (1, 4)), jnp.concatenate([-sin, -sin, sin, sin], axis=1)


def _pair_permutation():
    cols = []
    for pair in range(ATT_WIDTH // LANES):
        base = pair * LANES
        for quarter in (0, 2, 1, 3):
            cols.extend(range(base + quarter * HALF_ROT, base + (quarter + 1) * HALF_ROT))
    return jnp.asarray(cols, dtype=jnp.int32)


def _encoder(x, mod, cos_t, sin_t, g1, g2, gf, conv_w, conv_b, w_qkvhc, w_bgg, w_att, w_conv, w_mixo, w1, w2):
    outs = _qkv(x, mod, g1, cos_t, sin_t, w_qkvhc)
    att = _attn(outs[:3 * N_GROUPS])
    u = outs[3 * N_GROUPS]
    x1 = _mix(x, att, u, mod, g1, conv_w, conv_b, w_bgg, w_att, w_conv, w_mixo)
    return _mlp(x1, mod, g2, gf, w1, w2)


def kernel(x_prompt, x_sample, c_prompt, c_sample, w_ada, b_ada, norm1_g, w_in, conv_w, conv_b,
           w_attn_out, w_conv_out, w_mix_out, norm2_g, w_mlp_in, w_mlp_out, final_norm_g):
    assert w_ada.shape[0] == 1, "single layer"
    nb_p, nb_s = x_prompt.shape[0], x_sample.shape[0]
    pad = -(nb_p + nb_s) % 8
    c_all = jnp.concatenate([c_prompt, c_sample, jnp.zeros((pad, D_MODEL), F32)], axis=0)
    mod = _ada(c_all, w_ada[0], b_ada[0][None, :])
    mod = mod.reshape(mod.shape[0], 6, D_MODEL)

    w = w_in[0]
    perm = _pair_permutation()
    wq = (w[:, :ATT_WIDTH] * (HEAD_DIM ** -0.5))[:, perm]
    wk = w[:, ATT_WIDTH:2 * ATT_WIDTH][:, perm]
    split = 3 * ATT_WIDTH + 2 * D_MODEL
    w_qkvhc = jnp.concatenate([wq, wk, w[:, 2 * ATT_WIDTH:split]], axis=1).astype(BF16)
    w_bgg = w[:, split:].astype(BF16)
    shared = (norm1_g, norm2_g, final_norm_g[None, :], conv_w[0], conv_b, w_qkvhc, w_bgg,
              w_attn_out[0].astype(BF16), w_conv_out[0].astype(BF16), w_mix_out[0].astype(BF16),
              w_mlp_in[0].astype(BF16), w_mlp_out[0].astype(BF16))

    ys = []
    for x, m in ((x_prompt, mod[:nb_p]), (x_sample, mod[nb_p:nb_p + nb_s])):
        cos_t, sin_t = _rope_tables(x.shape[1])
        ys.append(_encoder(x, m, cos_t, sin_t, *shared))
    return tuple(ys)
```

```python
import functools

import jax
import jax.numpy as jnp
from jax import lax
from jax.experimental import pallas as pl
from jax.experimental.pallas import tpu as pltpu

D_MODEL = 1024
HEAD_DIM = 64
HEADS_PER_GROUP = 8
GROUP_WIDTH = HEADS_PER_GROUP * HEAD_DIM
DILATION_GROUPS = ((128, 1), (512, 4), (2048, 16))
DILS = tuple(d for _, d in DILATION_GROUPS)
BAND = DILATION_GROUPS[0][0] // (2 * DILATION_GROUPS[0][1])
assert all(w // (2 * d) == BAND for w, d in DILATION_GROUPS)
N_GROUPS = len(DILS)
ATT_WIDTH = N_GROUPS * GROUP_WIDTH
CONV_WIDTH = 3
D_FF = 4 * D_MODEL
ROPE_THETA = 10000.0
EPS = 1e-6
NEG = -1e30

LANES = 128
HALF_ROT = HEAD_DIM // 2

QKV_TILE = 512
ATT_TILE = 1024
ATT_Q_BLOCK = 2 * BAND
ATT_K_BLOCK = 4 * BAND
ATT_BLOCKS_PER_BODY = (2, 4, 2)
VMEM_LIMIT = 60 * 1024 * 1024

F32 = jnp.float32
BF16 = jnp.bfloat16


def _resident(shape):
    return pl.BlockSpec(shape, lambda *_: (0,) * len(shape), pipeline_mode=pl.Buffered(1))


def _params(n_axes):
    return pltpu.CompilerParams(
        dimension_semantics=("parallel",) * n_axes, vmem_limit_bytes=VMEM_LIMIT)


def _modulated_norm(x, gain, scale, shift):
    r = lax.rsqrt(jnp.mean(x * x, axis=-1, keepdims=True) + EPS)
    return (x * r) * (gain * (1.0 + scale)) + shift


def _ada_kernel(c_ref, w_ref, b_ref, o_ref):
    c = c_ref[...]
    s = c * jax.nn.sigmoid(c)
    o_ref[...] = jnp.dot(s, w_ref[...], precision=lax.Precision.HIGHEST,
                         preferred_element_type=F32) + b_ref[...]


def _ada(c_all, w_ada, b_ada):
    rows, _ = c_all.shape
    n_out = w_ada.shape[1]
    bn = n_out // 4
    return pl.pallas_call(
        _ada_kernel,
        grid=(n_out // bn,),
        in_specs=[
            pl.BlockSpec((rows, D_MODEL), lambda j: (0, 0)),
            pl.BlockSpec((D_MODEL, bn), lambda j: (0, j)),
            pl.BlockSpec((1, bn), lambda j: (0, j)),
        ],
        out_specs=pl.BlockSpec((rows, bn), lambda j: (0, j)),
        out_shape=jax.ShapeDtypeStruct((rows, n_out), F32),
        compiler_params=_params(1),
        name="ada",
    )(c_all, w_ada, b_ada)


def _qkv_kernel(x_ref, mod_ref, g1_ref, cos_ref, sin_ref, w_ref, *rest):
    qkv_refs = rest[:3 * N_GROUPS]
    u_ref = rest[3 * N_GROUPS]
    scr_ref = rest[3 * N_GROUPS + 1]
    tm = x_ref.shape[1]

    mod = mod_ref[0]
    xn = _modulated_norm(x_ref[0], g1_ref[...], mod[1:2], mod[0:1])
    xb = xn.astype(BF16)
    cos = cos_ref[...]
    sin = sin_ref[...]

    def emit(out_ref, dil, val):
        if dil == 1:
            out_ref[0, 0] = val.astype(BF16)
        else:
            for p in range(GROUP_WIDTH // LANES):
                scr_ref[p] = val[:, p * LANES:(p + 1) * LANES]
            for r in range(dil):
                for p in range(GROUP_WIDTH // LANES):
                    out_ref[0, r, :, p * LANES:(p + 1) * LANES] = (
                        scr_ref[p, pl.ds(r, tm // dil, stride=dil), :].astype(BF16))

    def rope(val):
        parts = []
        for p in range(GROUP_WIDTH // LANES):
            xp = val[:, p * LANES:(p + 1) * LANES]
            parts.append(xp * cos + pltpu.roll(xp, 2 * HALF_ROT, 1) * sin)
        return jnp.concatenate(parts, axis=1)

    for which in range(3):
        for g, dil in enumerate(DILS):
            col = which * ATT_WIDTH + g * GROUP_WIDTH
            val = jnp.dot(xb, w_ref[:, col:col + GROUP_WIDTH], preferred_element_type=F32)
            if which < 2:
                val = rope(val)
            emit(qkv_refs[3 * g + which], dil, val)

    col = 3 * ATT_WIDTH
    h = jnp.dot(xb, w_ref[:, col:col + D_MODEL], preferred_element_type=F32)
    cg = jnp.dot(xb, w_ref[:, col + D_MODEL:col + 2 * D_MODEL], preferred_element_type=F32)
    u_ref[0] = cg * h


def _qkv(x, mod, g1, cos_t, sin_t, w_qkvhc):
    nb, seq, _ = x.shape
    tm = QKV_TILE
    n_cols = w_qkvhc.shape[1]
    out_shapes = []
    out_specs = []
    for dil in DILS:
        for _ in range(3):
            out_shapes.append(jax.ShapeDtypeStruct((nb, dil, seq // dil, GROUP_WIDTH), BF16))
            out_specs.append(pl.BlockSpec((1, dil, tm // dil, GROUP_WIDTH), lambda b, i: (b, 0, i, 0)))
    out_shapes.append(jax.ShapeDtypeStruct((nb, seq, D_MODEL), F32))
    out_specs.append(pl.BlockSpec((1, tm, D_MODEL), lambda b, i: (b, i, 0)))
    return pl.pallas_call(
        _qkv_kernel,
        grid=(nb, seq // tm),
        in_specs=[
            pl.BlockSpec((1, tm, D_MODEL), lambda b, i: (b, i, 0)),
            pl.BlockSpec((1, 6, D_MODEL), lambda b, i: (b, 0, 0)),
            pl.BlockSpec((1, D_MODEL), lambda b, i: (0, 0)),
            pl.BlockSpec((tm, LANES), lambda b, i: (i, 0)),
            pl.BlockSpec((tm, LANES), lambda b, i: (i, 0)),
            _resident((D_MODEL, n_cols)),
        ],
        out_specs=out_specs,
        out_shape=out_shapes,
        scratch_shapes=[pltpu.VMEM((GROUP_WIDTH // LANES, tm, LANES), F32)],
        compiler_params=_params(2),
        name="qkv",
    )(x, mod, g1, cos_t, sin_t, w_qkvhc)


def _attn_kernel(*refs):
    in_refs = refs[:7 * N_GROUPS]
    att_ref = refs[7 * N_GROUPS]
    o_scr, lse_scr, s_scr, kf_scr, vf_scr, bias_scr = refs[7 * N_GROUPS + 1:]
    tm = att_ref.shape[1]
    i = pl.program_id(1)
    last = pl.num_programs(1) - 1

    n_pairs = GROUP_WIDTH // LANES
    lane = lax.broadcasted_iota(jnp.int32, (1, LANES), 1)
    qk_head_a = (lane // HALF_ROT) % 2 == 0
    qk_head_masks = (qk_head_a, jnp.logical_not(qk_head_a))
    v_head_a = lane < HEAD_DIM
    tk = ATT_K_BLOCK

    for g, dil in enumerate(DILS):
        q_ref, kp_ref, kc_ref, kn_ref, vp_ref, vc_ref, vn_ref = in_refs[7 * g:7 * g + 7]
        n = tm // dil
        tq = min(ATT_Q_BLOCK, n)
        nsub = n // tq
        per_body = ATT_BLOCKS_PER_BODY[g]
        n_iters = dil * nsub // per_body
        one_class = nsub >= per_body
        if one_class:
            assert dil == 1 and nsub % per_body == 0
        else:
            assert per_body % (2 * nsub) == 0 and dil % (per_body // nsub) == 0

        row = lax.broadcasted_iota(jnp.int32, (tq, tk), 0)
        colk = lax.broadcasted_iota(jnp.int32, (tq, tk), 1)
        rel = colk - BAND - row
        band = (rel >= -BAND) & (rel <= BAND)
        after_start = colk >= BAND
        before_end = colk < tq + BAND
        for variant, valid in enumerate((band, band & after_start, band & before_end,
                                         band & after_start & before_end)):
            bias_scr[variant, 0:tq] = jnp.where(valid, 0.0, NEG).astype(F32)

        def block_coords(it, j, nsub=nsub, per_body=per_body, one_class=one_class):
            if one_class:
                return 0, it * per_body + j, 0
            return it * (per_body // nsub) + j // nsub, j % nsub, (j // nsub) % 2

        def stage_kv(r, slot, n=n, tq=tq, kp_ref=kp_ref, kc_ref=kc_ref, kn_ref=kn_ref,
                     vp_ref=vp_ref, vc_ref=vc_ref, vn_ref=vn_ref):
            pad = tk - tq - 2 * BAND
            for full, before, cur, after in ((kf_scr, kp_ref, kc_ref, kn_ref),
                                             (vf_scr, vp_ref, vc_ref, vn_ref)):
                full[slot, 0:BAND] = before[0, r]
                full[slot, BAND:BAND + n] = cur[0, r]
                full[slot, BAND + n:2 * BAND + n] = after[0, r]
                if pad:
                    full[slot, 2 * BAND + n:2 * BAND + n + pad] = jnp.zeros((pad, GROUP_WIDTH), full.dtype)

        def scores(it, j, tq=tq, q_ref=q_ref, block_coords=block_coords, stage_kv=stage_kv,
                   one_class=one_class):
            r, sub, kv_slot = block_coords(it, j)
            if not one_class and sub == 0:
                stage_kv(r, kv_slot)
            q0 = sub * tq if isinstance(sub, int) else pl.multiple_of(sub * tq, tq)
            for p in range(n_pairs):
                cols = slice(p * LANES, (p + 1) * LANES)
                qp = q_ref[0, r, pl.ds(q0, tq), cols]
                kp = kf_scr[kv_slot, pl.ds(q0, tk), cols]
                for hh, mask in enumerate(qk_head_masks):
                    qh = jnp.where(mask, qp, jnp.zeros_like(qp))
                    s_scr[j % 2, 2 * p + hh, 0:tq] = lax.dot_general(
                        qh, kp, (((1,), (1,)), ((), ())), preferred_element_type=F32)

        def finish(it, j, tq=tq, nsub=nsub, dil=dil, g=g, block_coords=block_coords):
            r, sub, kv_slot = block_coords(it, j)
            q0 = sub * tq if isinstance(sub, int) else pl.multiple_of(sub * tq, tq)
            first = jnp.logical_and(i == 0, sub == 0)
            final = jnp.logical_and(i == last, sub == nsub - 1)
            bias = bias_scr[first.astype(jnp.int32) + 2 * final.astype(jnp.int32), 0:tq]
            for p in range(n_pairs):
                cols = slice(p * LANES, (p + 1) * LANES)
                vp = vf_scr[kv_slot, pl.ds(q0, tk), cols]
                outs = []
                for hh in range(2):
                    s = s_scr[j % 2, 2 * p + hh, 0:tq] + bias
                    m = jnp.max(s, axis=-1, keepdims=True)
                    e = jnp.exp(s - m)
                    den = jnp.sum(e, axis=-1, keepdims=True)
                    o = jnp.dot(e.astype(BF16), vp, preferred_element_type=F32)
                    outs.append((o, m, den))
                (oa, ma, da), (ob, mb, db) = outs
                o2 = jnp.where(v_head_a, oa, ob)
                m2 = jnp.where(v_head_a, ma, mb)
                d2 = jnp.where(v_head_a, da, db)
                if dil == 1:
                    rows = pl.ds(q0, tq)
                else:
                    rows = pl.ds(r + dil * q0, tq, stride=dil)
                o_scr[g * n_pairs + p, rows, :] = o2 / d2
                lse_scr[g * n_pairs + p, rows, :] = m2 + jnp.log(d2)

        if one_class:
            stage_kv(0, 0)
        scores(0, 0)

        def body(it, carry, per_body=per_body, n_iters=n_iters, scores=scores, finish=finish):
            for j in range(per_body):
                if j + 1 < per_body:
                    scores(it, j + 1)
                else:
                    scores(jnp.minimum(it + 1, n_iters - 1), 0)
                finish(it, j)
            return carry

        lax.fori_loop(0, n_iters, body, 0)

    chunk = 128

    def merge_body(c, carry):
        rows = pl.ds(pl.multiple_of(c * chunk, chunk), chunk)
        for p in range(n_pairs):
            ls = [lse_scr[g * n_pairs + p, rows, :] for g in range(N_GROUPS)]
            mx = functools.reduce(jnp.maximum, ls)
            ws = [jnp.exp(l - mx) for l in ls]
            num = sum(w * o_scr[g * n_pairs + p, rows, :] for g, w in enumerate(ws))
            att_ref[0, rows, p * LANES:(p + 1) * LANES] = (num / sum(ws)).astype(att_ref.dtype)
        return carry

    lax.fori_loop(0, tm // chunk, merge_body, 0)


def _attn(qkv):
    nb = qkv[0].shape[0]
    seq = qkv[0].shape[1] * qkv[0].shape[2]
    tm = ATT_TILE
    operands = []
    in_specs = []
    kv_rows = 0
    for g, dil in enumerate(DILS):
        q, k, v = qkv[3 * g:3 * g + 3]
        n = tm // dil
        nblk = n // BAND
        n_halo = seq // dil // BAND
        cur = pl.BlockSpec((1, dil, n, GROUP_WIDTH), lambda b, i: (b, 0, i, 0))
        prev = pl.BlockSpec((1, dil, BAND, GROUP_WIDTH),
                            lambda b, i, nblk=nblk: (b, 0, jnp.maximum(i * nblk - 1, 0), 0))
        nxt = pl.BlockSpec((1, dil, BAND, GROUP_WIDTH),
                           lambda b, i, nblk=nblk, n_halo=n_halo: (b, 0, jnp.minimum((i + 1) * nblk, n_halo - 1), 0))
        operands += [q, k, k, k, v, v, v]
        in_specs += [cur, prev, cur, nxt, prev, cur, nxt]
        kv_rows = max(kv_rows, n - min(ATT_Q_BLOCK, n) + ATT_K_BLOCK)
    merge_shape = (N_GROUPS * GROUP_WIDTH // LANES, tm, LANES)
    scratch = [
        pltpu.VMEM(merge_shape, F32),
        pltpu.VMEM(merge_shape, F32),
        pltpu.VMEM((2, HEADS_PER_GROUP, ATT_Q_BLOCK, ATT_K_BLOCK), F32),
        pltpu.VMEM((2, kv_rows, GROUP_WIDTH), BF16),
        pltpu.VMEM((2, kv_rows, GROUP_WIDTH), BF16),
        pltpu.VMEM((4, ATT_Q_BLOCK, ATT_K_BLOCK), F32),
    ]
    return pl.pallas_call(
        _attn_kernel,
        grid=(nb, seq // tm),
        in_specs=in_specs,
        out_specs=pl.BlockSpec((1, tm, GROUP_WIDTH), lambda b, i: (b, i, 0)),
        out_shape=jax.ShapeDtypeStruct((nb, seq, GROUP_WIDTH), BF16),
        scratch_shapes=scratch,
        compiler_params=_params(2),
        name="attn",
    )(*operands)


def _mix_kernel(x_ref, att_ref, u_ref, up_ref, un_ref, mod_ref, g1_ref, cw_ref, cb_ref,
                w3_ref, wa_ref, wc_ref, wm_ref, o_ref):
    tm = x_ref.shape[1]
    i = pl.program_id(1)
    last = pl.num_programs(1) - 1
    mod = mod_ref[0]
    x = x_ref[0]
    xb = _modulated_norm(x, g1_ref[...], mod[1:2], mod[0:1]).astype(BF16)

    u = u_ref[0]
    halo = up_ref.shape[1]
    u_before = jnp.where(i > 0, up_ref[0, halo - 1:halo, :], 0.0)
    u_after = jnp.where(i < last, un_ref[0, 0:1, :], 0.0)
    rows = lax.broadcasted_iota(jnp.int32, (tm, 1), 0)
    u_m1 = jnp.where(rows == 0, u_before, pltpu.roll(u, 1, 0))
    u_p1 = jnp.where(rows == tm - 1, u_after, pltpu.roll(u, tm - 1, 0))
    cw = cw_ref[...]
    conv = cb_ref[...] + u_m1 * cw[0:1] + u * cw[1:2] + u_p1 * cw[2:3]

    b_gate = jnp.dot(xb, w3_ref[:, 0:D_MODEL], preferred_element_type=F32)
    cv = jnp.dot((b_gate * conv).astype(BF16), wc_ref[...], preferred_element_type=F32)
    g_att = jnp.dot(xb, w3_ref[:, D_MODEL:2 * D_MODEL], preferred_element_type=F32)
    ap = jnp.dot(att_ref[0], wa_ref[...], preferred_element_type=F32)
    merged = jax.nn.sigmoid(g_att) * ap
    g_conv = jnp.dot(xb, w3_ref[:, 2 * D_MODEL:3 * D_MODEL], preferred_element_type=F32)
    merged = merged + jax.nn.sigmoid(g_conv) * cv
    mix = jnp.dot(merged.astype(BF16), wm_ref[...], preferred_element_type=F32)
    o_ref[0] = x + mod[2:3] * mix


def _mix(x, att, u, mod, g1, conv_w, conv_b, w_bgg, w_att, w_conv, w_mixo):
    nb, seq, _ = x.shape
    tm = QKV_TILE
    halo = 8
    nh = tm // halo
    return pl.pallas_call(
        _mix_kernel,
        grid=(nb, seq // tm),
        in_specs=[
            pl.BlockSpec((1, tm, D_MODEL), lambda b, i: (b, i, 0)),
            pl.BlockSpec((1, tm, GROUP_WIDTH), lambda b, i: (b, i, 0)),
            pl.BlockSpec((1, tm, D_MODEL), lambda b, i: (b, i, 0)),
            pl.BlockSpec((1, halo, D_MODEL), lambda b, i: (b, jnp.maximum(i * nh - 1, 0), 0)),
            pl.BlockSpec((1, halo, D_MODEL), lambda b, i: (b, jnp.minimum((i + 1) * nh, seq // halo - 1), 0)),
            pl.BlockSpec((1, 6, D_MODEL), lambda b, i: (b, 0, 0)),
            pl.BlockSpec((1, D_MODEL), lambda b, i: (0, 0)),
            pl.BlockSpec((CONV_WIDTH, D_MODEL), lambda b, i: (0, 0)),
            pl.BlockSpec((1, D_MODEL), lambda b, i: (0, 0)),
            _resident(w_bgg.shape),
            _resident(w_att.shape),
            _resident(w_conv.shape),
            _resident(w_mixo.shape),
        ],
        out_specs=pl.BlockSpec((1, tm, D_MODEL), lambda b, i: (b, i, 0)),
        out_shape=jax.ShapeDtypeStruct((nb, seq, D_MODEL), F32),
        compiler_params=_params(2),
        name="mix",
    )(x, att, u, u, u, mod, g1, conv_w, conv_b, w_bgg, w_att, w_conv, w_mixo)


def _mlp_kernel(x_ref, mod_ref, g2_ref, gf_ref, w1_ref, w2_ref, o_ref):
    mod = mod_ref[0]
    x = x_ref[0]
    xb = _modulated_norm(x, g2_ref[...], mod[4:5], mod[3:4]).astype(BF16)
    ff_chunk = D_MODEL
    acc = jnp.zeros(x.shape, F32)
    for c in range(D_FF // ff_chunk):
        cols = slice(c * ff_chunk, (c + 1) * ff_chunk)
        h = jnp.dot(xb, w1_ref[:, cols], preferred_element_type=F32)
        h = jnp.square(jnp.maximum(h, 0.0)).astype(BF16)
        acc = acc + jnp.dot(h, w2_ref[cols, :], preferred_element_type=F32)
    x2 = x + mod[5:6] * acc
    r = lax.rsqrt(jnp.mean(x2 * x2, axis=-1, keepdims=True) + EPS)
    o_ref[0] = (x2 * r) * gf_ref[...]


def _mlp(x1, mod, g2, gf, w1, w2):
    nb, seq, _ = x1.shape
    tm = QKV_TILE
    return pl.pallas_call(
        _mlp_kernel,
        grid=(nb, seq // tm),
        in_specs=[
            pl.BlockSpec((1, tm, D_MODEL), lambda b, i: (b, i, 0)),
            pl.BlockSpec((1, 6, D_MODEL), lambda b, i: (b, 0, 0)),
            pl.BlockSpec((1, D_MODEL), lambda b, i: (0, 0)),
            pl.BlockSpec((1, D_MODEL), lambda b, i: (0, 0)),
            _resident(w1.shape),
            _resident(w2.shape),
        ],
        out_specs=pl.BlockSpec((1, tm, D_MODEL), lambda b, i: (b, i, 0)),
        out_shape=jax.ShapeDtypeStruct((nb, seq, D_MODEL), F32),
        compiler_params=_params(2),
        name="mlp",
    )(x1, mod, g2, gf, w1, w2)


def _rope_tables(seq):
    inv = 1.0 / (ROPE_THETA ** (jnp.arange(HALF_ROT, dtype=F32) / HALF_ROT))
    ang = jnp.arange(seq).astype(F32)[:, None] * inv[None, :]
    cos = jnp.cos(ang)
    sin = jnp.sin(ang)
    return jnp.tile(cos, (1, 4)), jnp.concatenate([-sin, -sin, sin, sin], axis=1)


def _pair_permutation():
    cols = []
    for pair in range(ATT_WIDTH // LANES):
        base = pair * LANES
        for quarter in (0, 2, 1, 3):
            cols.extend(range(base + quarter * HALF_ROT, base + (quarter + 1) * HALF_ROT))
    return jnp.asarray(cols, dtype=jnp.int32)


def _encoder(x, mod, cos_t, sin_t, g1, g2, gf, conv_w, conv_b, w_qkvhc, w_bgg, w_att, w_conv, w_mixo, w1, w2):
    outs = _qkv(x, mod, g1, cos_t, sin_t, w_qkvhc)
    att = _attn(outs[:3 * N_GROUPS])
    u = outs[3 * N_GROUPS]
    x1 = _mix(x, att, u, mod, g1, conv_w, conv_b, w_bgg, w_att, w_conv, w_mixo)
    return _mlp(x1, mod, g2, gf, w1, w2)


def kernel(x_prompt, x_sample, c_prompt, c_sample, w_ada, b_ada, norm1_g, w_in, conv_w, conv_b,
           w_attn_out, w_conv_out, w_mix_out, norm2_g, w_mlp_in, w_mlp_out, final_norm_g):
    assert w_ada.shape[0] == 1, "single layer"
    nb_p, nb_s = x_prompt.shape[0], x_sample.shape[0]
    pad = -(nb_p + nb_s) % 8
    c_all = jnp.concatenate([c_prompt, c_sample, jnp.zeros((pad, D_MODEL), F32)], axis=0)
    mod = _ada(c_all, w_ada[0], b_ada[0][None, :])
    mod = mod.reshape(mod.shape[0], 6, D_MODEL)

    w = w_in[0]
    perm = _pair_permutation()
    wq = (w[:, :ATT_WIDTH] * (HEAD_DIM ** -0.5))[:, perm]
    wk = w[:, ATT_WIDTH:2 * ATT_WIDTH][:, perm]
    split = 3 * ATT_WIDTH + 2 * D_MODEL
    w_qkvhc = jnp.concatenate([wq, wk, w[:, 2 * ATT_WIDTH:split]], axis=1).astype(BF16)
    w_bgg = w[:, split:].astype(BF16)
    shared = (norm1_g, norm2_g, final_norm_g[None, :], conv_w[0], conv_b, w_qkvhc, w_bgg,
              w_attn_out[0].astype(BF16), w_conv_out[0].astype(BF16), w_mix_out[0].astype(BF16),
              w_mlp_in[0].astype(BF16), w_mlp_out[0].astype(BF16))

    ys = []
    for x, m in ((x_prompt, mod[:nb_p]), (x_sample, mod[nb_p:nb_p + nb_s])):
        cos_t, sin_t = _rope_tables(x.shape[1])
        ys.append(_encoder(x, m, cos_t, sin_t, *shared))
    return tuple(ys)
```

```python
import functools

import jax
import jax.numpy as jnp
import numpy as np
from jax import lax
from jax.experimental import pallas as pl
from jax.experimental.pallas import tpu as pltpu

D_MODEL = 1024
HEAD_DIM = 64
HEADS_PER_GROUP = 8
GROUP_WIDTH = HEADS_PER_GROUP * HEAD_DIM
DILATION_GROUPS = ((128, 1), (512, 4), (2048, 16))
DILS = tuple(d for _, d in DILATION_GROUPS)
BAND = DILATION_GROUPS[0][0] // (2 * DILATION_GROUPS[0][1])
assert all(w // (2 * d) == BAND for w, d in DILATION_GROUPS)
N_GROUPS = len(DILS)
ATT_WIDTH = N_GROUPS * GROUP_WIDTH
CONV_WIDTH = 3
D_FF = 4 * D_MODEL
ROPE_THETA = 10000.0
EPS = 1e-6
NEG = -1e30

LANES = 128
HALF_ROT = HEAD_DIM // 2
QK_SCALE = HEAD_DIM ** -0.5

QKV_TILE = 512
ATT_TILE = 1024
ATT_Q_BLOCK = 2 * BAND
ATT_K_BLOCK = 4 * BAND
ATT_BLOCKS_PER_BODY = (4, 4, 8)
VMEM_LIMIT = 60 * 1024 * 1024

F32 = jnp.float32
BF16 = jnp.bfloat16


def _resident(shape):
    return pl.BlockSpec(shape, lambda *_: (0,) * len(shape), pipeline_mode=pl.Buffered(1))


def _params(n_axes):
    return pltpu.CompilerParams(
        dimension_semantics=("parallel",) * n_axes, vmem_limit_bytes=VMEM_LIMIT)


def _modulated_norm(x, gain, scale, shift):
    r = lax.rsqrt(jnp.mean(x * x, axis=-1, keepdims=True) + EPS)
    return (x * r) * (gain * (1.0 + scale)) + shift


def _ada_kernel(c_ref, w_ref, b_ref, o_ref):
    c = c_ref[...]
    s = c * jax.nn.sigmoid(c)
    o_ref[...] = jnp.dot(s, w_ref[...], precision=lax.Precision.HIGHEST,
                         preferred_element_type=F32) + b_ref[...]


def _ada(c_all, w_ada, b_ada):
    rows, _ = c_all.shape
    n_out = w_ada.shape[1]
    bn = n_out // 4
    return pl.pallas_call(
        _ada_kernel,
        grid=(n_out // bn,),
        in_specs=[
            pl.BlockSpec((rows, D_MODEL), lambda j: (0, 0)),
            pl.BlockSpec((D_MODEL, bn), lambda j: (0, j)),
            pl.BlockSpec((1, bn), lambda j: (0, j)),
        ],
        out_specs=pl.BlockSpec((rows, bn), lambda j: (0, j)),
        out_shape=jax.ShapeDtypeStruct((rows, n_out), F32),
        compiler_params=_params(1),
        name="ada",
    )(c_all, w_ada, b_ada)


def _qkv_kernel(x_ref, mod_ref, g1_ref, rope_base_ref, rope_tile_ref, w_ref, *rest):
    qkv_refs = rest[:3 * N_GROUPS]
    u_ref = rest[3 * N_GROUPS]
    scr_ref = rest[3 * N_GROUPS + 1]
    tm = x_ref.shape[1]

    mod = mod_ref[0]
    xn = _modulated_norm(x_ref[0], g1_ref[...], mod[1:2], mod[0:1])
    xb = xn.astype(BF16)
    cb, sb, ssb = rope_base_ref[0], rope_base_ref[1], rope_base_ref[2]
    ta = rope_tile_ref[0]
    ca, sa, ssa = ta[0:1], ta[1:2], ta[2:3]
    cos = ca * cb - sa * sb
    sin = ssa * cb + ca * ssb
    lane = lax.broadcasted_iota(jnp.int32, (1, LANES), 1)
    first_half = (lane // HALF_ROT) % 2 == 0

    def emit(out_ref, dil, val):
        if dil == 1:
            out_ref[0, 0] = val.astype(BF16)
        else:
            for p in range(GROUP_WIDTH // LANES):
                scr_ref[p] = val[:, p * LANES:(p + 1) * LANES]
            for r in range(dil):
                for p in range(GROUP_WIDTH // LANES):
                    out_ref[0, r, :, p * LANES:(p + 1) * LANES] = (
                        scr_ref[p, pl.ds(r, tm // dil, stride=dil), :].astype(BF16))

    def rope(val, scale):
        parts = []
        for p in range(GROUP_WIDTH // LANES):
            xp = val[:, p * LANES:(p + 1) * LANES]
            partner = jnp.where(first_half, pltpu.roll(xp, LANES - HALF_ROT, 1), pltpu.roll(xp, HALF_ROT, 1))
            parts.append(xp * (cos * scale) + partner * (sin * scale))
        return jnp.concatenate(parts, axis=1)

    for which in range(3):
        for g, dil in enumerate(DILS):
            col = which * ATT_WIDTH + g * GROUP_WIDTH
            val = jnp.dot(xb, w_ref[:, col:col + GROUP_WIDTH], preferred_element_type=F32)
            if which < 2:
                val = rope(val, QK_SCALE if which == 0 else 1.0)
            emit(qkv_refs[3 * g + which], dil, val)

    col = 3 * ATT_WIDTH
    h = jnp.dot(xb, w_ref[:, col:col + D_MODEL], preferred_element_type=F32)
    cg = jnp.dot(xb, w_ref[:, col + D_MODEL:col + 2 * D_MODEL], preferred_element_type=F32)
    u_ref[0] = cg * h


def _qkv(x, mod, g1, w_in):
    nb, seq, _ = x.shape
    tm = QKV_TILE
    n_cols = 3 * ATT_WIDTH + 2 * D_MODEL
    rope_base, rope_tile = _rope_tables(seq, tm)
    out_shapes = []
    out_specs = []
    for dil in DILS:
        for _ in range(3):
            out_shapes.append(jax.ShapeDtypeStruct((nb, dil, seq // dil, GROUP_WIDTH), BF16))
            out_specs.append(pl.BlockSpec((1, dil, tm // dil, GROUP_WIDTH), lambda b, i: (b, 0, i, 0)))
    out_shapes.append(jax.ShapeDtypeStruct((nb, seq, D_MODEL), F32))
    out_specs.append(pl.BlockSpec((1, tm, D_MODEL), lambda b, i: (b, i, 0)))
    return pl.pallas_call(
        _qkv_kernel,
        grid=(nb, seq // tm),
        in_specs=[
            pl.BlockSpec((1, tm, D_MODEL), lambda b, i: (b, i, 0)),
            pl.BlockSpec((1, 6, D_MODEL), lambda b, i: (b, 0, 0)),
            pl.BlockSpec((1, D_MODEL), lambda b, i: (0, 0)),
            pl.BlockSpec((3, tm, LANES), lambda b, i: (0, 0, 0)),
            pl.BlockSpec((1, 3, LANES), lambda b, i: (i, 0, 0)),
            pl.BlockSpec((D_MODEL, n_cols), lambda b, i: (0, 0), pipeline_mode=pl.Buffered(1)),
        ],
        out_specs=out_specs,
        out_shape=out_shapes,
        scratch_shapes=[pltpu.VMEM((GROUP_WIDTH // LANES, tm, LANES), F32)],
        compiler_params=_params(2),
        name="qkv",
    )(x, mod, g1, rope_base, rope_tile, w_in)


def _attn_kernel(*refs):
    in_refs = refs[:7 * N_GROUPS]
    att_ref = refs[7 * N_GROUPS]
    o_scr, lse_scr, s_scr, m_scr, kf_scr, vf_scr, bias_scr = refs[7 * N_GROUPS + 1:]
    tm = att_ref.shape[1]
    i = pl.program_id(1)
    last = pl.num_programs(1) - 1

    n_pairs = GROUP_WIDTH // LANES
    lane = lax.broadcasted_iota(jnp.int32, (1, LANES), 1)
    head_a = lane < HEAD_DIM
    head_masks = (head_a, jnp.logical_not(head_a))
    tk = ATT_K_BLOCK

    for g, dil in enumerate(DILS):
        q_ref, kp_ref, kc_ref, kn_ref, vp_ref, vc_ref, vn_ref = in_refs[7 * g:7 * g + 7]
        n = tm // dil
        tq = min(ATT_Q_BLOCK, n)
        nsub = n // tq
        per_body = ATT_BLOCKS_PER_BODY[g]
        n_iters = dil * nsub // per_body
        one_class = nsub >= per_body
        if one_class:
            assert dil == 1 and nsub % per_body == 0
        else:
            assert per_body % (2 * nsub) == 0 and dil % (per_body // nsub) == 0

        row = lax.broadcasted_iota(jnp.int32, (tq, tk), 0)
        colk = lax.broadcasted_iota(jnp.int32, (tq, tk), 1)
        rel = colk - BAND - row
        band = (rel >= -BAND) & (rel <= BAND)
        after_start = colk >= BAND
        before_end = colk < tq + BAND
        for variant, valid in enumerate((band, band & after_start, band & before_end,
                                         band & after_start & before_end)):
            bias_scr[variant, 0:tq] = jnp.where(valid, 0.0, NEG).astype(F32)

        def block_coords(it, j, nsub=nsub, per_body=per_body, one_class=one_class):
            if one_class:
                return 0, it * per_body + j, 0
            return it * (per_body // nsub) + j // nsub, j % nsub, (j // nsub) % 2

        def stage_kv(r, slot, n=n, tq=tq, kp_ref=kp_ref, kc_ref=kc_ref, kn_ref=kn_ref,
                     vp_ref=vp_ref, vc_ref=vc_ref, vn_ref=vn_ref):
            pad = tk - tq - 2 * BAND
            for full, before, cur, after in ((kf_scr, kp_ref, kc_ref, kn_ref),
                                             (vf_scr, vp_ref, vc_ref, vn_ref)):
                full[slot, 0:BAND] = before[0, r]
                full[slot, BAND:BAND + n] = cur[0, r]
                full[slot, BAND + n:2 * BAND + n] = after[0, r]
                if pad:
                    full[slot, 2 * BAND + n:2 * BAND + n + pad] = jnp.zeros((pad, GROUP_WIDTH), full.dtype)

        def edge_bias(sub, tq=tq, nsub=nsub):
            first = jnp.logical_and(i == 0, sub == 0)
            final = jnp.logical_and(i == last, sub == nsub - 1)
            return bias_scr[first.astype(jnp.int32) + 2 * final.astype(jnp.int32), 0:tq]

        def scores(it, j, tq=tq, q_ref=q_ref, block_coords=block_coords, stage_kv=stage_kv,
                   one_class=one_class, edge_bias=edge_bias):
            r, sub, kv_slot = block_coords(it, j)
            if not one_class and sub == 0:
                stage_kv(r, kv_slot)
            q0 = sub * tq if isinstance(sub, int) else pl.multiple_of(sub * tq, tq)
            bias = edge_bias(sub)
            for p in range(n_pairs):
                cols = slice(p * LANES, (p + 1) * LANES)
                qp = q_ref[0, r, pl.ds(q0, tq), cols]
                kp = kf_scr[kv_slot, pl.ds(q0, tk), cols]
                for hh, mask in enumerate(head_masks):
                    qh = jnp.where(mask, qp, jnp.zeros_like(qp))
                    s = lax.dot_general(qh, kp, (((1,), (1,)), ((), ())), preferred_element_type=F32)
                    s = s + bias
                    s_scr[j % 2, 2 * p + hh, 0:tq] = s
                    m_scr[j % 2, 2 * p + hh, 0:tq] = jnp.broadcast_to(
                        jnp.max(s, axis=-1, keepdims=True), (tq, LANES))

        def finish(it, j, tq=tq, dil=dil, g=g, block_coords=block_coords):
            r, sub, kv_slot = block_coords(it, j)
            q0 = sub * tq if isinstance(sub, int) else pl.multiple_of(sub * tq, tq)
            for p in range(n_pairs):
                cols = slice(p * LANES, (p + 1) * LANES)
                vp = vf_scr[kv_slot, pl.ds(q0, tk), cols]
                zeros = jnp.zeros_like(vp)
                ones_a = jnp.broadcast_to(jnp.where(head_a, 1.0, 0.0).astype(BF16), vp.shape)
                ones_b = jnp.broadcast_to(jnp.where(head_a, 0.0, 1.0).astype(BF16), vp.shape)
                rhs = (jnp.concatenate([jnp.where(head_a, vp, zeros), ones_a], axis=1),
                       jnp.concatenate([jnp.where(head_a, zeros, vp), ones_b], axis=1))
                acc = None
                ms = []
                for hh in range(2):
                    s = s_scr[j % 2, 2 * p + hh, 0:tq]
                    m = m_scr[j % 2, 2 * p + hh, 0:tq]
                    e = jnp.concatenate([jnp.exp(s[:, t * LANES:(t + 1) * LANES] - m)
                                         for t in range(tk // LANES)], axis=1)
                    part = jnp.dot(e.astype(BF16), rhs[hh], preferred_element_type=F32)
                    acc = part if acc is None else acc + part
                    ms.append(m)
                o2 = acc[:, 0:LANES]
                d2 = acc[:, LANES:2 * LANES]
                m2 = jnp.where(head_a, ms[0], ms[1])
                if dil == 1:
                    rows = pl.ds(q0, tq)
                else:
                    rows = pl.ds(r + dil * q0, tq, stride=dil)
                o_scr[g * n_pairs + p, rows, :] = o2 / d2
                lse_scr[g * n_pairs + p, rows, :] = m2 + jnp.log(d2)

        if one_class:
            stage_kv(0, 0)
        scores(0, 0)

        def body(it, carry, per_body=per_body, n_iters=n_iters, scores=scores, finish=finish):
            for j in range(per_body):
                if j + 1 < per_body:
                    scores(it, j + 1)
                else:
                    scores(jnp.minimum(it + 1, n_iters - 1), 0)
                finish(it, j)
            return carry

        lax.fori_loop(0, n_iters, body, 0)

    chunk = 128

    def merge_body(c, carry):
        rows = pl.ds(pl.multiple_of(c * chunk, chunk), chunk)
        for p in range(n_pairs):
            ls = [lse_scr[g * n_pairs + p, rows, :] for g in range(N_GROUPS)]
            mx = functools.reduce(jnp.maximum, ls)
            ws = [jnp.exp(l - mx) for l in ls]
            num = sum(w * o_scr[g * n_pairs + p, rows, :] for g, w in enumerate(ws))
            att_ref[0, rows, p * LANES:(p + 1) * LANES] = (num / sum(ws)).astype(att_ref.dtype)
        return carry

    lax.fori_loop(0, tm // chunk, merge_body, 0)


def _attn(qkv):
    nb = qkv[0].shape[0]
    seq = qkv[0].shape[1] * qkv[0].shape[2]
    tm = ATT_TILE
    operands = []
    in_specs = []
    kv_rows = 0
    for g, dil in enumerate(DILS):
        q, k, v = qkv[3 * g:3 * g + 3]
        n = tm // dil
        nblk = n // BAND
        n_halo = seq // dil // BAND
        cur = pl.BlockSpec((1, dil, n, GROUP_WIDTH), lambda b, i: (b, 0, i, 0))
        prev = pl.BlockSpec((1, dil, BAND, GROUP_WIDTH),
                            lambda b, i, nblk=nblk: (b, 0, jnp.maximum(i * nblk - 1, 0), 0))
        nxt = pl.BlockSpec((1, dil, BAND, GROUP_WIDTH),
                           lambda b, i, nblk=nblk, n_halo=n_halo: (b, 0, jnp.minimum((i + 1) * nblk, n_halo - 1), 0))
        operands += [q, k, k, k, v, v, v]
        in_specs += [cur, prev, cur, nxt, prev, cur, nxt]
        kv_rows = max(kv_rows, n - min(ATT_Q_BLOCK, n) + ATT_K_BLOCK)
    merge_shape = (N_GROUPS * GROUP_WIDTH // LANES, tm, LANES)
    scratch = [
        pltpu.VMEM(merge_shape, F32),
        pltpu.VMEM(merge_shape, F32),
        pltpu.VMEM((2, HEADS_PER_GROUP, ATT_Q_BLOCK, ATT_K_BLOCK), F32),
        pltpu.VMEM((2, HEADS_PER_GROUP, ATT_Q_BLOCK, LANES), F32),
        pltpu.VMEM((2, kv_rows, GROUP_WIDTH), BF16),
        pltpu.VMEM((2, kv_rows, GROUP_WIDTH), BF16),
        pltpu.VMEM((4, ATT_Q_BLOCK, ATT_K_BLOCK), F32),
    ]
    return pl.pallas_call(
        _attn_kernel,
        grid=(nb, seq // tm),
        in_specs=in_specs,
        out_specs=pl.BlockSpec((1, tm, GROUP_WIDTH), lambda b, i: (b, i, 0)),
        out_shape=jax.ShapeDtypeStruct((nb, seq, GROUP_WIDTH), BF16),
        scratch_shapes=scratch,
        compiler_params=_params(2),
        name="attn",
    )(*operands)


def _mix_kernel(x_ref, att_ref, u_ref, up_ref, un_ref, mod_ref, g1_ref, cw_ref, cb_ref,
                w3_ref, wa_ref, wc_ref, wm_ref, o_ref):
    tm = x_ref.shape[1]
    i = pl.program_id(1)
    last = pl.num_programs(1) - 1
    mod = mod_ref[0]
    x = x_ref[0]
    xb = _modulated_norm(x, g1_ref[...], mod[1:2], mod[0:1]).astype(BF16)

    u = u_ref[0]
    halo = up_ref.shape[1]
    u_before = jnp.where(i > 0, up_ref[0, halo - 1:halo, :], 0.0)
    u_after = jnp.where(i < last, un_ref[0, 0:1, :], 0.0)
    rows = lax.broadcasted_iota(jnp.int32, (tm, 1), 0)
    u_m1 = jnp.where(rows == 0, u_before, pltpu.roll(u, 1, 0))
    u_p1 = jnp.where(rows == tm - 1, u_after, pltpu.roll(u, tm - 1, 0))
    cw = cw_ref[...]
    conv = cb_ref[...] + u_m1 * cw[0:1] + u * cw[1:2] + u_p1 * cw[2:3]

    b_gate = jnp.dot(xb, w3_ref[:, 0:D_MODEL], preferred_element_type=F32)
    cv = jnp.dot((b_gate * conv).astype(BF16), wc_ref[...], preferred_element_type=F32)
    g_att = jnp.dot(xb, w3_ref[:, D_MODEL:2 * D_MODEL], preferred_element_type=F32)
    ap = jnp.dot(att_ref[0], wa_ref[...], preferred_element_type=F32)
    merged = jax.nn.sigmoid(g_att) * ap
    g_conv = jnp.dot(xb, w3_ref[:, 2 * D_MODEL:3 * D_MODEL], preferred_element_type=F32)
    merged = merged + jax.nn.sigmoid(g_conv) * cv
    mix = jnp.dot(merged.astype(BF16), wm_ref[...], preferred_element_type=F32)
    o_ref[0] = x + mod[2:3] * mix


def _mix(x, att, u, mod, g1, conv_w, conv_b, w_bgg, w_att, w_conv, w_mixo):
    nb, seq, _ = x.shape
    tm = QKV_TILE
    halo = 8
    nh = tm // halo
    return pl.pallas_call(
        _mix_kernel,
        grid=(nb, seq // tm),
        in_specs=[
            pl.BlockSpec((1, tm, D_MODEL), lambda b, i: (b, i, 0)),
            pl.BlockSpec((1, tm, GROUP_WIDTH), lambda b, i: (b, i, 0)),
            pl.BlockSpec((1, tm, D_MODEL), lambda b, i: (b, i, 0)),
            pl.BlockSpec((1, halo, D_MODEL), lambda b, i: (b, jnp.maximum(i * nh - 1, 0), 0)),
            pl.BlockSpec((1, halo, D_MODEL), lambda b, i: (b, jnp.minimum((i + 1) * nh, seq // halo - 1), 0)),
            pl.BlockSpec((1, 6, D_MODEL), lambda b, i: (b, 0, 0)),
            pl.BlockSpec((1, D_MODEL), lambda b, i: (0, 0)),
            pl.BlockSpec((CONV_WIDTH, D_MODEL), lambda b, i: (0, 0)),
            pl.BlockSpec((1, D_MODEL), lambda b, i: (0, 0)),
            _resident(w_bgg.shape),
            _resident(w_att.shape),
            _resident(w_conv.shape),
            _resident(w_mixo.shape),
        ],
        out_specs=pl.BlockSpec((1, tm, D_MODEL), lambda b, i: (b, i, 0)),
        out_shape=jax.ShapeDtypeStruct((nb, seq, D_MODEL), F32),
        compiler_params=_params(2),
        name="mix",
    )(x, att, u, u, u, mod, g1, conv_w, conv_b, w_bgg, w_att, w_conv, w_mixo)


def _mlp_kernel(x_ref, mod_ref, g2_ref, gf_ref, w1_ref, w2_ref, o_ref):
    mod = mod_ref[0]
    x = x_ref[0]
    xb = _modulated_norm(x, g2_ref[...], mod[4:5], mod[3:4]).astype(BF16)
    ff_chunk = D_MODEL
    acc = jnp.zeros(x.shape, F32)
    for c in range(D_FF // ff_chunk):
        cols = slice(c * ff_chunk, (c + 1) * ff_chunk)
        h = jnp.dot(xb, w1_ref[:, cols], preferred_element_type=F32)
        h = jnp.square(jnp.maximum(h, 0.0)).astype(BF16)
        acc = acc + jnp.dot(h, w2_ref[cols, :], preferred_element_type=F32)
    x2 = x + mod[5:6] * acc
    r = lax.rsqrt(jnp.mean(x2 * x2, axis=-1, keepdims=True) + EPS)
    o_ref[0] = (x2 * r) * gf_ref[...]


def _mlp(x1, mod, g2, gf, w1, w2):
    nb, seq, _ = x1.shape
    tm = QKV_TILE
    return pl.pallas_call(
        _mlp_kernel,
        grid=(nb, seq // tm),
        in_specs=[
            pl.BlockSpec((1, tm, D_MODEL), lambda b, i: (b, i, 0)),
            pl.BlockSpec((1, 6, D_MODEL), lambda b, i: (b, 0, 0)),
            pl.BlockSpec((1, D_MODEL), lambda b, i: (0, 0)),
            pl.BlockSpec((1, D_MODEL), lambda b, i: (0, 0)),
            _resident(w1.shape),
            _resident(w2.shape),
        ],
        out_specs=pl.BlockSpec((1, tm, D_MODEL), lambda b, i: (b, i, 0)),
        out_shape=jax.ShapeDtypeStruct((nb, seq, D_MODEL), F32),
        compiler_params=_params(2),
        name="mlp",
    )(x1, mod, g2, gf, w1, w2)


def _rope_tables(seq, tm):
    inv = 1.0 / (ROPE_THETA ** (np.arange(HALF_ROT, dtype=np.float64) / HALF_ROT))
    inv = np.tile(inv, LANES // HALF_ROT)[None, :]
    sign = np.where((np.arange(LANES) // HALF_ROT) % 2 == 0, -1.0, 1.0)[None, :]
    ang_b = np.arange(tm, dtype=np.float64)[:, None] * inv
    ang_a = (np.arange(seq // tm, dtype=np.float64) * tm)[:, None] * inv
    base = np.stack([np.cos(ang_b), np.sin(ang_b), sign * np.sin(ang_b)])
    tile = np.stack([np.cos(ang_a), np.sin(ang_a), sign * np.sin(ang_a)], axis=1)
    return jnp.asarray(base, F32), jnp.asarray(tile, F32)


def _encoder(x, mod, g1, g2, gf, conv_w, conv_b, w_in, w_bgg, w_att, w_conv, w_mixo, w1, w2):
    outs = _qkv(x, mod, g1, w_in)
    att = _attn(outs[:3 * N_GROUPS])
    u = outs[3 * N_GROUPS]
    x1 = _mix(x, att, u, mod, g1, conv_w, conv_b, w_bgg, w_att, w_conv, w_mixo)
    return _mlp(x1, mod, g2, gf, w1, w2)


def kernel(x_prompt, x_sample, c_prompt, c_sample, w_ada, b_ada, norm1_g, w_in, conv_w, conv_b,
           w_attn_out, w_conv_out, w_mix_out, norm2_g, w_mlp_in, w_mlp_out, final_norm_g):
    assert w_ada.shape[0] == 1, "single layer"
    nb_p, nb_s = x_prompt.shape[0], x_sample.shape[0]
    pad = -(nb_p + nb_s) % 8
    c_all = jnp.concatenate([c_prompt, c_sample, jnp.zeros((pad, D_MODEL), F32)], axis=0)
    mod = _ada(c_all, w_ada[0], b_ada[0][None, :])
    mod = mod.reshape(mod.shape[0], 6, D_MODEL)

    w_in_b = w_in[0].astype(BF16)
    w_bgg = w_in_b[:, 3 * ATT_WIDTH + 2 * D_MODEL:]
    shared = (norm1_g, norm2_g, final_norm_g[None, :], conv_w[0], conv_b, w_in_b, w_bgg,
              w_attn_out[0].astype(BF16), w_conv_out[0].astype(BF16), w_mix_out[0].astype(BF16),
              w_mlp_in[0].astype(BF16), w_mlp_out[0].astype(BF16))

    ys = []
    for x, m in ((x_prompt, mod[:nb_p]), (x_sample, mod[nb_p:nb_p + nb_s])):
        ys.append(_encoder(x, m, *shared))
    return tuple(ys)
```

```python
import functools

import jax
import jax.numpy as jnp
import numpy as np
from jax import lax
from jax.experimental import pallas as pl
from jax.experimental.pallas import tpu as pltpu

D_MODEL = 1024
HEAD_DIM = 64
HEADS_PER_GROUP = 8
GROUP_WIDTH = HEADS_PER_GROUP * HEAD_DIM
DILATION_GROUPS = ((128, 1), (512, 4), (2048, 16))
DILS = tuple(d for _, d in DILATION_GROUPS)
BAND = DILATION_GROUPS[0][0] // (2 * DILATION_GROUPS[0][1])
assert all(w // (2 * d) == BAND for w, d in DILATION_GROUPS)
N_GROUPS = len(DILS)
ATT_WIDTH = N_GROUPS * GROUP_WIDTH
CONV_WIDTH = 3
D_FF = 4 * D_MODEL
ROPE_THETA = 10000.0
EPS = 1e-6
NEG = -1e30

LANES = 128
SUBLANES = 8
HALF_ROT = HEAD_DIM // 2
QK_SCALE = HEAD_DIM ** -0.5 * np.log2(np.e)

QKV_TILE = 512
ROW_PARTS = 2
ATT_TILE = 1024
ATT_Q_BLOCK = 2 * BAND
ATT_K_BLOCK = 4 * BAND
ATT_BLOCKS_PER_BODY = (4, 4, 8)
VMEM_LIMIT = 60 * 1024 * 1024

F32 = jnp.float32
BF16 = jnp.bfloat16


def _resident(shape):
    return pl.BlockSpec(shape, lambda *_: (0,) * len(shape), pipeline_mode=pl.Buffered(1))


def _params(n_axes):
    return pltpu.CompilerParams(
        dimension_semantics=("parallel",) * n_axes, vmem_limit_bytes=VMEM_LIMIT)


def _modulated_norm(x, gain, scale, shift):
    r = lax.rsqrt(jnp.mean(x * x, axis=-1, keepdims=True) + EPS)
    return (x * r) * (gain * (1.0 + scale)) + shift


def _ada_kernel(c_ref, w_ref, b_ref, o_ref):
    c = c_ref[...]
    s = c * jax.nn.sigmoid(c)
    o_ref[...] = jnp.dot(s, w_ref[...], precision=lax.Precision.HIGHEST,
                         preferred_element_type=F32) + b_ref[...]


def _ada(c_all, w_ada, b_ada):
    rows, _ = c_all.shape
    n_out = w_ada.shape[1]
    bn = n_out // 4
    return pl.pallas_call(
        _ada_kernel,
        grid=(n_out // bn,),
        in_specs=[
            pl.BlockSpec((rows, D_MODEL), lambda j: (0, 0)),
            pl.BlockSpec((D_MODEL, bn), lambda j: (0, j)),
            pl.BlockSpec((1, bn), lambda j: (0, j)),
        ],
        out_specs=pl.BlockSpec((rows, bn), lambda j: (0, j)),
        out_shape=jax.ShapeDtypeStruct((rows, n_out), F32),
        compiler_params=_params(1),
        name="ada",
    )(c_all, w_ada, b_ada)


def _qkv_kernel(x_ref, mod_ref, g1_ref, rope_base_ref, rope_tile_ref, w_ref, *rest):
    qkv_refs = rest[:3 * N_GROUPS]
    u_ref = rest[3 * N_GROUPS]
    scr_ref = rest[3 * N_GROUPS + 1]
    tm = x_ref.shape[1]
    mod = mod_ref[0]
    ta = rope_tile_ref[0]
    ca, sa, ssa = ta[0:1], ta[1:2], ta[2:3]
    lane = lax.broadcasted_iota(jnp.int32, (1, LANES), 1)
    first_half = (lane // HALF_ROT) % 2 == 0

    for part in range(ROW_PARTS):
        rp = tm // ROW_PARTS
        rows = slice(part * rp, (part + 1) * rp)
        xn = _modulated_norm(x_ref[0, rows], g1_ref[...], mod[1:2], mod[0:1])
        xb = xn.astype(BF16)
        cb, sb, ssb = rope_base_ref[0, rows], rope_base_ref[1, rows], rope_base_ref[2, rows]
        cos = ca * cb - sa * sb
        sin = ssa * cb + ca * ssb

        def emit(out_ref, dil, val, slot, part=part, rp=rp):
            if dil == 1:
                out_ref[0, 0, part * rp:(part + 1) * rp] = val.astype(BF16)
            else:
                n = rp // dil
                for p in range(GROUP_WIDTH // LANES):
                    scr_ref[part, slot, p] = val[:, p * LANES:(p + 1) * LANES]
                for r in range(dil):
                    for p in range(GROUP_WIDTH // LANES):
                        out_ref[0, r, part * n:(part + 1) * n, p * LANES:(p + 1) * LANES] = (
                            scr_ref[part, slot, p, pl.ds(r, n, stride=dil), :].astype(BF16))

        def rope(val, scale, cos=cos, sin=sin):
            parts = []
            for p in range(GROUP_WIDTH // LANES):
                xp = val[:, p * LANES:(p + 1) * LANES]
                partner = jnp.where(first_half, pltpu.roll(xp, LANES - HALF_ROT, 1),
                                    pltpu.roll(xp, HALF_ROT, 1))
                parts.append(xp * (cos * scale) + partner * (sin * scale))
            return jnp.concatenate(parts, axis=1)

        for which in range(3):
            for g, dil in enumerate(DILS):
                col = which * ATT_WIDTH + g * GROUP_WIDTH
                val = jnp.dot(xb, w_ref[:, col:col + GROUP_WIDTH], preferred_element_type=F32)
                if which < 2:
                    val = rope(val, QK_SCALE if which == 0 else 1.0)
                emit(qkv_refs[3 * g + which], dil, val, slot=which * (N_GROUPS - 1) + g - 1)

        col = 3 * ATT_WIDTH
        h = jnp.dot(xb, w_ref[:, col:col + D_MODEL], preferred_element_type=F32)
        cg = jnp.dot(xb, w_ref[:, col + D_MODEL:col + 2 * D_MODEL], preferred_element_type=F32)
        u_ref[0, rows] = cg * h


def _qkv(x, mod, g1, w_in):
    nb, seq, _ = x.shape
    tm = QKV_TILE
    n_cols = 3 * ATT_WIDTH + 2 * D_MODEL
    rope_base, rope_tile = _rope_tables(seq, tm)
    out_shapes = []
    out_specs = []
    for dil in DILS:
        for _ in range(3):
            out_shapes.append(jax.ShapeDtypeStruct((nb, dil, seq // dil, GROUP_WIDTH), BF16))
            out_specs.append(pl.BlockSpec((1, dil, tm // dil, GROUP_WIDTH), lambda b, i: (b, 0, i, 0)))
    out_shapes.append(jax.ShapeDtypeStruct((nb, seq, D_MODEL), F32))
    out_specs.append(pl.BlockSpec((1, tm, D_MODEL), lambda b, i: (b, i, 0)))
    return pl.pallas_call(
        _qkv_kernel,
        grid=(nb, seq // tm),
        in_specs=[
            pl.BlockSpec((1, tm, D_MODEL), lambda b, i: (b, i, 0)),
            pl.BlockSpec((1, 6, D_MODEL), lambda b, i: (b, 0, 0)),
            pl.BlockSpec((1, D_MODEL), lambda b, i: (0, 0)),
            pl.BlockSpec((3, tm, LANES), lambda b, i: (0, 0, 0)),
            pl.BlockSpec((1, 3, LANES), lambda b, i: (i, 0, 0)),
            pl.BlockSpec((D_MODEL, n_cols), lambda b, i: (0, 0), pipeline_mode=pl.Buffered(1)),
        ],
        out_specs=out_specs,
        out_shape=out_shapes,
        scratch_shapes=[pltpu.VMEM((ROW_PARTS, 3 * (N_GROUPS - 1), GROUP_WIDTH // LANES,
                                    tm // ROW_PARTS, LANES), F32)],
        compiler_params=_params(2),
        name="qkv",
    )(x, mod, g1, rope_base, rope_tile, w_in)


def _attn_kernel(*refs):
    in_refs = refs[:7 * N_GROUPS]
    att_ref = refs[7 * N_GROUPS]
    (o_scr, lse_scr, kf_scr, vf_scr, st_scr, mt_scr, biast_scr,
     sr_scr, mr_scr, biasr_scr) = refs[7 * N_GROUPS + 1:]
    tm = att_ref.shape[1]
    i = pl.program_id(1)
    last = pl.num_programs(1) - 1

    n_pairs = GROUP_WIDTH // LANES
    lane = lax.broadcasted_iota(jnp.int32, (1, LANES), 1)
    head_a = lane < HEAD_DIM
    tk = ATT_K_BLOCK

    for g, dil in enumerate(DILS):
        q_ref, kp_ref, kc_ref, kn_ref, vp_ref, vc_ref, vn_ref = in_refs[7 * g:7 * g + 7]
        n = tm // dil
        tq = min(ATT_Q_BLOCK, n)
        nsub = n // tq
        per_body = ATT_BLOCKS_PER_BODY[g]
        n_iters = dil * nsub // per_body
        one_class = nsub >= per_body
        if one_class:
            assert dil == 1 and nsub % per_body == 0
        else:
            assert per_body % (2 * nsub) == 0 and dil % (per_body // nsub) == 0

        transposed = tq == LANES
        s_scr, m_scr, bias_scr = (st_scr, mt_scr, biast_scr) if transposed else (sr_scr, mr_scr, biasr_scr)

        if transposed:
            key = lax.broadcasted_iota(jnp.int32, (tk, tq), 0)
            qry = lax.broadcasted_iota(jnp.int32, (tk, tq), 1)
        else:
            qry = lax.broadcasted_iota(jnp.int32, (tq, tk), 0)
            key = lax.broadcasted_iota(jnp.int32, (tq, tk), 1)
        rel = key - BAND - qry
        band = (rel >= -BAND) & (rel <= BAND)
        after_start = key >= BAND
        before_end = key < tq + BAND
        for variant, valid in enumerate((band, band & after_start, band & before_end,
                                         band & after_start & before_end)):
            bias_scr[variant] = jnp.where(valid, 0.0, NEG).astype(F32)

        def block_coords(it, j, nsub=nsub, per_body=per_body, one_class=one_class):
            if one_class:
                return 0, it * per_body + j, 0
            return it * (per_body // nsub) + j // nsub, j % nsub, (j // nsub) % 2

        def stage_kv(r, slot, n=n, tq=tq, kp_ref=kp_ref, kc_ref=kc_ref, kn_ref=kn_ref,
                     vp_ref=vp_ref, vc_ref=vc_ref, vn_ref=vn_ref):
            pad = tk - tq - 2 * BAND
            for full, before, cur, after in ((kf_scr, kp_ref, kc_ref, kn_ref),
                                             (vf_scr, vp_ref, vc_ref, vn_ref)):
                full[slot, 0:BAND] = before[0, r]
                full[slot, BAND:BAND + n] = cur[0, r]
                full[slot, BAND + n:2 * BAND + n] = after[0, r]
                if pad:
                    full[slot, 2 * BAND + n:2 * BAND + n + pad] = jnp.zeros((pad, GROUP_WIDTH), full.dtype)

        def edge_bias(sub, nsub=nsub, bias_scr=bias_scr):
            first = jnp.logical_and(i == 0, sub == 0)
            final = jnp.logical_and(i == last, sub == nsub - 1)
            return bias_scr[first.astype(jnp.int32) + 2 * final.astype(jnp.int32)]

        def scores(it, j, tq=tq, q_ref=q_ref, block_coords=block_coords, stage_kv=stage_kv,
                   one_class=one_class, edge_bias=edge_bias, transposed=transposed,
                   s_scr=s_scr, m_scr=m_scr):
            r, sub, kv_slot = block_coords(it, j)
            if not one_class and sub == 0:
                stage_kv(r, kv_slot)
            q0 = sub * tq if isinstance(sub, int) else pl.multiple_of(sub * tq, tq)
            bias = edge_bias(sub)
            for p in range(n_pairs):
                cols = slice(p * LANES, (p + 1) * LANES)
                qp = q_ref[0, r, pl.ds(q0, tq), cols]
                kp = kf_scr[kv_slot, pl.ds(q0, tk), cols]
                zeros = jnp.zeros_like(qp)
                q_heads = (jnp.where(head_a, qp, zeros), jnp.where(head_a, zeros, qp))
                if transposed:
                    s = lax.dot_general(kp, jnp.concatenate(q_heads, axis=0),
                                        (((1,), (1,)), ((), ())), preferred_element_type=F32)
                    s = jnp.concatenate([s[:, 0:tq] + bias, s[:, tq:2 * tq] + bias], axis=1)
                    s_scr[j % 2, p] = s
                    m_scr[j % 2, p] = jnp.broadcast_to(jnp.max(s, axis=0, keepdims=True),
                                                       (SUBLANES, 2 * tq))
                else:
                    for hh in range(2):
                        s = lax.dot_general(q_heads[hh], kp, (((1,), (1,)), ((), ())),
                                            preferred_element_type=F32) + bias
                        s_scr[j % 2, 2 * p + hh] = s
                        m_scr[j % 2, 2 * p + hh] = jnp.broadcast_to(
                            jnp.max(s, axis=-1, keepdims=True), (tq, LANES))

        def finish(it, j, tq=tq, dil=dil, g=g, block_coords=block_coords, transposed=transposed,
                   s_scr=s_scr, m_scr=m_scr):
            r, sub, kv_slot = block_coords(it, j)
            q0 = sub * tq if isinstance(sub, int) else pl.multiple_of(sub * tq, tq)
            for p in range(n_pairs):
                cols = slice(p * LANES, (p + 1) * LANES)
                vp = vf_scr[kv_slot, pl.ds(q0, tk), cols]
                if transposed:
                    vt = vp.T
                    ones = jnp.ones((2 * SUBLANES, tk), BF16)
                    m = m_scr[j % 2, p]
                    e = jnp.exp2(s_scr[j % 2, p] - m[0:1]).astype(BF16)
                    tops, lses = [], []
                    for hh in range(2):
                        lhs = jnp.concatenate([vt[hh * HEAD_DIM:(hh + 1) * HEAD_DIM], ones], axis=0)
                        res = jnp.dot(lhs, e[:, hh * tq:(hh + 1) * tq], preferred_element_type=F32)
                        den = res[HEAD_DIM:HEAD_DIM + SUBLANES]
                        tops.append(res[0:HEAD_DIM] * jnp.tile(1.0 / den, (HEAD_DIM // SUBLANES, 1)))
                        lse = m[:, hh * tq:(hh + 1) * tq] + jnp.log2(den)
                        lses.append(jnp.tile(lse, (HEAD_DIM // SUBLANES, 1)))
                    o2 = jnp.concatenate(tops, axis=0).T
                    lse2 = jnp.concatenate(lses, axis=0).T
                else:
                    zeros = jnp.zeros_like(vp)
                    ones_a = jnp.broadcast_to(jnp.where(head_a, 1.0, 0.0).astype(BF16), vp.shape)
                    ones_b = jnp.broadcast_to(jnp.where(head_a, 0.0, 1.0).astype(BF16), vp.shape)
                    rhs = (jnp.concatenate([jnp.where(head_a, vp, zeros), ones_a], axis=1),
                           jnp.concatenate([jnp.where(head_a, zeros, vp), ones_b], axis=1))
                    acc = None
                    ms = []
                    for hh in range(2):
                        s = s_scr[j % 2, 2 * p + hh]
                        m = m_scr[j % 2, 2 * p + hh]
                        e = jnp.concatenate([jnp.exp2(s[:, t * LANES:(t + 1) * LANES] - m)
                                             for t in range(tk // LANES)], axis=1)
                        part = jnp.dot(e.astype(BF16), rhs[hh], preferred_element_type=F32)
                        acc = part if acc is None else acc + part
                        ms.append(m)
                    den = acc[:, LANES:2 * LANES]
                    o2 = acc[:, 0:LANES] / den
                    lse2 = jnp.where(head_a, ms[0], ms[1]) + jnp.log2(den)
                if dil == 1:
                    rows = pl.ds(q0, tq)
                else:
                    rows = pl.ds(r + dil * q0, tq, stride=dil)
                o_scr[g * n_pairs + p, rows, :] = o2
                lse_scr[g * n_pairs + p, rows, :] = lse2

        if one_class:
            stage_kv(0, 0)
        scores(0, 0)

        def body(it, carry, per_body=per_body, n_iters=n_iters, scores=scores, finish=finish):
            for j in range(per_body):
                if j + 1 < per_body:
                    scores(it, j + 1)
                else:
                    scores(jnp.minimum(it + 1, n_iters - 1), 0)
                finish(it, j)
            return carry

        lax.fori_loop(0, n_iters, body, 0)

    chunk = 128

    def merge_body(c, carry):
        rows = pl.ds(pl.multiple_of(c * chunk, chunk), chunk)
        for p in range(n_pairs):
            ls = [lse_scr[g * n_pairs + p, rows, :] for g in range(N_GROUPS)]
            mx = functools.reduce(jnp.maximum, ls)
            ws = [jnp.exp2(l - mx) for l in ls]
            num = sum(w * o_scr[g * n_pairs + p, rows, :] for g, w in enumerate(ws))
            att_ref[0, rows, p * LANES:(p + 1) * LANES] = (num / sum(ws)).astype(att_ref.dtype)
        return carry

    lax.fori_loop(0, tm // chunk, merge_body, 0)


def _attn(qkv):
    nb = qkv[0].shape[0]
    seq = qkv[0].shape[1] * qkv[0].shape[2]
    tm = ATT_TILE
    operands = []
    in_specs = []
    kv_rows = 0
    for g, dil in enumerate(DILS):
        q, k, v = qkv[3 * g:3 * g + 3]
        n = tm // dil
        nblk = n // BAND
        n_halo = seq // dil // BAND
        cur = pl.BlockSpec((1, dil, n, GROUP_WIDTH), lambda b, i: (b, 0, i, 0))
        prev = pl.BlockSpec((1, dil, BAND, GROUP_WIDTH),
                            lambda b, i, nblk=nblk: (b, 0, jnp.maximum(i * nblk - 1, 0), 0))
        nxt = pl.BlockSpec((1, dil, BAND, GROUP_WIDTH),
                           lambda b, i, nblk=nblk, n_halo=n_halo: (b, 0, jnp.minimum((i + 1) * nblk, n_halo - 1), 0))
        operands += [q, k, k, k, v, v, v]
        in_specs += [cur, prev, cur, nxt, prev, cur, nxt]
        kv_rows = max(kv_rows, n - min(ATT_Q_BLOCK, n) + ATT_K_BLOCK)
    merge_shape = (N_GROUPS * GROUP_WIDTH // LANES, tm, LANES)
    tq_small = min(ATT_Q_BLOCK, tm // max(DILS))
    scratch = [
        pltpu.VMEM(merge_shape, F32),
        pltpu.VMEM(merge_shape, F32),
        pltpu.VMEM((2, kv_rows, GROUP_WIDTH), BF16),
        pltpu.VMEM((2, kv_rows, GROUP_WIDTH), BF16),
        pltpu.VMEM((2, GROUP_WIDTH // LANES, ATT_K_BLOCK, 2 * ATT_Q_BLOCK), F32),
        pltpu.VMEM((2, GROUP_WIDTH // LANES, SUBLANES, 2 * ATT_Q_BLOCK), F32),
        pltpu.VMEM((4, ATT_K_BLOCK, ATT_Q_BLOCK), F32),
        pltpu.VMEM((2, HEADS_PER_GROUP, tq_small, ATT_K_BLOCK), F32),
        pltpu.VMEM((2, HEADS_PER_GROUP, tq_small, LANES), F32),
        pltpu.VMEM((4, tq_small, ATT_K_BLOCK), F32),
    ]
    return pl.pallas_call(
        _attn_kernel,
        grid=(nb, seq // tm),
        in_specs=in_specs,
        out_specs=pl.BlockSpec((1, tm, GROUP_WIDTH), lambda b, i: (b, i, 0)),
        out_shape=jax.ShapeDtypeStruct((nb, seq, GROUP_WIDTH), BF16),
        scratch_shapes=scratch,
        compiler_params=_params(2),
        name="attn",
    )(*operands)


def _mix_kernel(x_ref, att_ref, u_ref, up_ref, un_ref, mod_ref, g1_ref, cw_ref, cb_ref,
                w3_ref, wa_ref, wc_ref, wm_ref, o_ref):
    tm = x_ref.shape[1]
    i = pl.program_id(1)
    last = pl.num_programs(1) - 1
    mod = mod_ref[0]
    halo = up_ref.shape[1]
    cw = cw_ref[...]
    rp = tm // ROW_PARTS
    row_id = lax.broadcasted_iota(jnp.int32, (rp, 1), 0)

    for part in range(ROW_PARTS):
        rows = slice(part * rp, (part + 1) * rp)
        x = x_ref[0, rows]
        xb = _modulated_norm(x, g1_ref[...], mod[1:2], mod[0:1]).astype(BF16)

        u = u_ref[0, rows]
        if part == 0:
            u_before = jnp.where(i > 0, up_ref[0, halo - 1:halo, :], 0.0)
        else:
            u_before = u_ref[0, part * rp - 1:part * rp, :]
        if part == ROW_PARTS - 1:
            u_after = jnp.where(i < last, un_ref[0, 0:1, :], 0.0)
        else:
            u_after = u_ref[0, (part + 1) * rp:(part + 1) * rp + 1, :]
        u_m1 = jnp.where(row_id == 0, u_before, pltpu.roll(u, 1, 0))
        u_p1 = jnp.where(row_id == rp - 1, u_after, pltpu.roll(u, rp - 1, 0))
        conv = cb_ref[...] + u_m1 * cw[0:1] + u * cw[1:2] + u_p1 * cw[2:3]

        b_gate = jnp.dot(xb, w3_ref[:, 0:D_MODEL], preferred_element_type=F32)
        cv = jnp.dot((b_gate * conv).astype(BF16), wc_ref[...], preferred_element_type=F32)
        g_att = jnp.dot(xb, w3_ref[:, D_MODEL:2 * D_MODEL], preferred_element_type=F32)
        ap = jnp.dot(att_ref[0, rows], wa_ref[...], preferred_element_type=F32)
        merged = jax.nn.sigmoid(g_att) * ap
        g_conv = jnp.dot(xb, w3_ref[:, 2 * D_MODEL:3 * D_MODEL], preferred_element_type=F32)
        merged = merged + jax.nn.sigmoid(g_conv) * cv
        mix = jnp.dot(merged.astype(BF16), wm_ref[...], preferred_element_type=F32)
        o_ref[0, rows] = x + mod[2:3] * mix


def _mix(x, att, u, mod, g1, conv_w, conv_b, w_bgg, w_att, w_conv, w_mixo):
    nb, seq, _ = x.shape
    tm = QKV_TILE
    halo = 8
    nh = tm // halo
    return pl.pallas_call(
        _mix_kernel,
        grid=(nb, seq // tm),
        in_specs=[
            pl.BlockSpec((1, tm, D_MODEL), lambda b, i: (b, i, 0)),
            pl.BlockSpec((1, tm, GROUP_WIDTH), lambda b, i: (b, i, 0)),
            pl.BlockSpec((1, tm, D_MODEL), lambda b, i: (b, i, 0)),
            pl.BlockSpec((1, halo, D_MODEL), lambda b, i: (b, jnp.maximum(i * nh - 1, 0), 0)),
            pl.BlockSpec((1, halo, D_MODEL), lambda b, i: (b, jnp.minimum((i + 1) * nh, seq // halo - 1), 0)),
            pl.BlockSpec((1, 6, D_MODEL), lambda b, i: (b, 0, 0)),
            pl.BlockSpec((1, D_MODEL), lambda b, i: (0, 0)),
            pl.BlockSpec((CONV_WIDTH, D_MODEL), lambda b, i: (0, 0)),
            pl.BlockSpec((1, D_MODEL), lambda b, i: (0, 0)),
            _resident(w_bgg.shape),
            _resident(w_att.shape),
            _resident(w_conv.shape),
            _resident(w_mixo.shape),
        ],
        out_specs=pl.BlockSpec((1, tm, D_MODEL), lambda b, i: (b, i, 0)),
        out_shape=jax.ShapeDtypeStruct((nb, seq, D_MODEL), F32),
        compiler_params=_params(2),
        name="mix",
    )(x, att, u, u, u, mod, g1, conv_w, conv_b, w_bgg, w_att, w_conv, w_mixo)


def _mlp_kernel(x_ref, mod_ref, g2_ref, gf_ref, w1_ref, w2_ref, o_ref):
    mod = mod_ref[0]
    tm = x_ref.shape[1]
    rp = tm // ROW_PARTS
    ff_chunk = D_MODEL
    for part in range(ROW_PARTS):
        rows = slice(part * rp, (part + 1) * rp)
        x = x_ref[0, rows]
        xb = _modulated_norm(x, g2_ref[...], mod[4:5], mod[3:4]).astype(BF16)
        acc = jnp.zeros(x.shape, F32)
        for c in range(D_FF // ff_chunk):
            cols = slice(c * ff_chunk, (c + 1) * ff_chunk)
            h = jnp.dot(xb, w1_ref[:, cols], preferred_element_type=F32)
            h = jnp.square(jnp.maximum(h, 0.0)).astype(BF16)
            acc = acc + jnp.dot(h, w2_ref[cols, :], preferred_element_type=F32)
        x2 = x + mod[5:6] * acc
        r = lax.rsqrt(jnp.mean(x2 * x2, axis=-1, keepdims=True) + EPS)
        o_ref[0, rows] = (x2 * r) * gf_ref[...]


def _mlp(x1, mod, g2, gf, w1, w2):
    nb, seq, _ = x1.shape
    tm = QKV_TILE
    return pl.pallas_call(
        _mlp_kernel,
        grid=(nb, seq // tm),
        in_specs=[
            pl.BlockSpec((1, tm, D_MODEL), lambda b, i: (b, i, 0)),
            pl.BlockSpec((1, 6, D_MODEL), lambda b, i: (b, 0, 0)),
            pl.BlockSpec((1, D_MODEL), lambda b, i: (0, 0)),
            pl.BlockSpec((1, D_MODEL), lambda b, i: (0, 0)),
            _resident(w1.shape),
            _resident(w2.shape),
        ],
        out_specs=pl.BlockSpec((1, tm, D_MODEL), lambda b, i: (b, i, 0)),
        out_shape=jax.ShapeDtypeStruct((nb, seq, D_MODEL), F32),
        compiler_params=_params(2),
        name="mlp",
    )(x1, mod, g2, gf, w1, w2)


def _rope_tables(seq, tm):
    inv = 1.0 / (ROPE_THETA ** (np.arange(HALF_ROT, dtype=np.float64) / HALF_ROT))
    inv = np.tile(inv, LANES // HALF_ROT)[None, :]
    sign = np.where((np.arange(LANES) // HALF_ROT) % 2 == 0, -1.0, 1.0)[None, :]
    ang_b = np.arange(tm, dtype=np.float64)[:, None] * inv
    ang_a = (np.arange(seq // tm, dtype=np.float64) * tm)[:, None] * inv
    base = np.stack([np.cos(ang_b), np.sin(ang_b), sign * np.sin(ang_b)])
    tile = np.stack([np.cos(ang_a), np.sin(ang_a), sign * np.sin(ang_a)], axis=1)
    return jnp.asarray(base, F32), jnp.asarray(tile, F32)


def _encoder(x, mod, g1, g2, gf, conv_w, conv_b, w_in, w_bgg, w_att, w_conv, w_mixo, w1, w2):
    outs = _qkv(x, mod, g1, w_in)
    att = _attn(outs[:3 * N_GROUPS])
    u = outs[3 * N_GROUPS]
    x1 = _mix(x, att, u, mod, g1, conv_w, conv_b, w_bgg, w_att, w_conv, w_mixo)
    return _mlp(x1, mod, g2, gf, w1, w2)


def kernel(x_prompt, x_sample, c_prompt, c_sample, w_ada, b_ada, norm1_g, w_in, conv_w, conv_b,
           w_attn_out, w_conv_out, w_mix_out, norm2_g, w_mlp_in, w_mlp_out, final_norm_g):
    assert w_ada.shape[0] == 1, "single layer"
    nb_p, nb_s = x_prompt.shape[0], x_sample.shape[0]
    pad = -(nb_p + nb_s) % 8
    c_all = jnp.concatenate([c_prompt, c_sample, jnp.zeros((pad, D_MODEL), F32)], axis=0)
    mod = _ada(c_all, w_ada[0], b_ada[0][None, :])
    mod = mod.reshape(mod.shape[0], 6, D_MODEL)

    w_in_b = w_in[0].astype(BF16)
    w_bgg = w_in_b[:, 3 * ATT_WIDTH + 2 * D_MODEL:]
    shared = (norm1_g, norm2_g, final_norm_g[None, :], conv_w[0], conv_b, w_in_b, w_bgg,
              w_attn_out[0].astype(BF16), w_conv_out[0].astype(BF16), w_mix_out[0].astype(BF16),
              w_mlp_in[0].astype(BF16), w_mlp_out[0].astype(BF16))

    ys = []
    for x, m in ((x_prompt, mod[:nb_p]), (x_sample, mod[nb_p:nb_p + nb_s])):
        ys.append(_encoder(x, m, *shared))
    return tuple(ys)
```

```python
import functools

import jax
import jax.numpy as jnp
import numpy as np
from jax import lax
from jax.experimental import pallas as pl
from jax.experimental.pallas import tpu as pltpu

D_MODEL = 1024
HEAD_DIM = 64
HEADS_PER_GROUP = 8
GROUP_WIDTH = HEADS_PER_GROUP * HEAD_DIM
DILATION_GROUPS = ((128, 1), (512, 4), (2048, 16))
DILS = tuple(d for _, d in DILATION_GROUPS)
BAND = DILATION_GROUPS[0][0] // (2 * DILATION_GROUPS[0][1])
assert all(w // (2 * d) == BAND for w, d in DILATION_GROUPS)
N_GROUPS = len(DILS)
ATT_WIDTH = N_GROUPS * GROUP_WIDTH
CONV_WIDTH = 3
D_FF = 4 * D_MODEL
ROPE_THETA = 10000.0
EPS = 1e-6
NEG = -1e30

LANES = 128
SUBLANES = 8
HALF_ROT = HEAD_DIM // 2
QK_SCALE = HEAD_DIM ** -0.5 * np.log2(np.e)

QKV_TILE = 512
ROW_PARTS = 2
ATT_TILE = 1024
ATT_Q_BLOCK = 2 * BAND
ATT_K_BLOCK = 4 * BAND
ATT_BLOCKS_PER_BODY = (4, 4, 8)
VMEM_LIMIT = 60 * 1024 * 1024

F32 = jnp.float32
BF16 = jnp.bfloat16


def _resident(shape):
    return pl.BlockSpec(shape, lambda *_: (0,) * len(shape), pipeline_mode=pl.Buffered(1))


def _params(n_axes, semantics="parallel"):
    return pltpu.CompilerParams(
        dimension_semantics=(semantics,) * n_axes, vmem_limit_bytes=VMEM_LIMIT)


def _modulated_norm(x, gain, scale, shift):
    r = lax.rsqrt(jnp.mean(x * x, axis=-1, keepdims=True) + EPS)
    return (x * r) * (gain * (1.0 + scale)) + shift


def _ada_kernel(c_ref, w_ref, b_ref, o_ref):
    c = c_ref[...]
    s = c * jax.nn.sigmoid(c)
    o_ref[...] = jnp.dot(s, w_ref[...], precision=lax.Precision.HIGHEST,
                         preferred_element_type=F32) + b_ref[...]


def _ada(c_all, w_ada, b_ada):
    rows, _ = c_all.shape
    n_out = w_ada.shape[1]
    bn = n_out // 4
    return pl.pallas_call(
        _ada_kernel,
        grid=(n_out // bn,),
        in_specs=[
            pl.BlockSpec((rows, D_MODEL), lambda j: (0, 0)),
            pl.BlockSpec((D_MODEL, bn), lambda j: (0, j)),
            pl.BlockSpec((1, bn), lambda j: (0, j)),
        ],
        out_specs=pl.BlockSpec((rows, bn), lambda j: (0, j)),
        out_shape=jax.ShapeDtypeStruct((rows, n_out), F32),
        compiler_params=_params(1),
        name="ada",
    )(c_all, w_ada, b_ada)


def _qkv_kernel(x_ref, mod_ref, g1_ref, rope_base_ref, rope_tile_ref, w_ref, *rest):
    qkv_refs = rest[:3 * N_GROUPS]
    u_ref = rest[3 * N_GROUPS]
    scr_ref = rest[3 * N_GROUPS + 1]
    tm = x_ref.shape[1]
    mod = mod_ref[0]
    ta = rope_tile_ref[0]
    ca, sa, ssa = ta[0:1], ta[1:2], ta[2:3]
    lane = lax.broadcasted_iota(jnp.int32, (1, LANES), 1)
    first_half = (lane // HALF_ROT) % 2 == 0

    for part in range(ROW_PARTS):
        rp = tm // ROW_PARTS
        rows = slice(part * rp, (part + 1) * rp)
        xn = _modulated_norm(x_ref[0, rows], g1_ref[...], mod[1:2], mod[0:1])
        xb = xn.astype(BF16)
        cb, sb, ssb = rope_base_ref[0, rows], rope_base_ref[1, rows], rope_base_ref[2, rows]
        cos = ca * cb - sa * sb
        sin = ssa * cb + ca * ssb

        def emit(out_ref, dil, val, slot, part=part, rp=rp):
            if dil == 1:
                out_ref[0, 0, part * rp:(part + 1) * rp] = val.astype(BF16)
            else:
                n = rp // dil
                for p in range(GROUP_WIDTH // LANES):
                    scr_ref[part, slot, p] = val[:, p * LANES:(p + 1) * LANES]
                for r in range(dil):
                    for p in range(GROUP_WIDTH // LANES):
                        out_ref[0, r, part * n:(part + 1) * n, p * LANES:(p + 1) * LANES] = (
                            scr_ref[part, slot, p, pl.ds(r, n, stride=dil), :].astype(BF16))

        def rope(val, scale, cos=cos, sin=sin):
            parts = []
            for p in range(GROUP_WIDTH // LANES):
                xp = val[:, p * LANES:(p + 1) * LANES]
                partner = jnp.where(first_half, pltpu.roll(xp, LANES - HALF_ROT, 1),
                                    pltpu.roll(xp, HALF_ROT, 1))
                parts.append(xp * (cos * scale) + partner * (sin * scale))
            return jnp.concatenate(parts, axis=1)

        for which in range(3):
            for g, dil in enumerate(DILS):
                col = which * ATT_WIDTH + g * GROUP_WIDTH
                val = jnp.dot(xb, w_ref[:, col:col + GROUP_WIDTH], preferred_element_type=F32)
                if which < 2:
                    val = rope(val, QK_SCALE if which == 0 else 1.0)
                emit(qkv_refs[3 * g + which], dil, val, slot=which * (N_GROUPS - 1) + g - 1)

        col = 3 * ATT_WIDTH
        h = jnp.dot(xb, w_ref[:, col:col + D_MODEL], preferred_element_type=F32)
        cg = jnp.dot(xb, w_ref[:, col + D_MODEL:col + 2 * D_MODEL], preferred_element_type=F32)
        u_ref[0, rows] = cg * h


def _qkv(x, mod, g1, w_in):
    nb, seq, _ = x.shape
    tm = QKV_TILE
    n_cols = 3 * ATT_WIDTH + 2 * D_MODEL
    rope_base, rope_tile = _rope_tables(seq, tm)
    out_shapes = []
    out_specs = []
    for dil in DILS:
        for _ in range(3):
            out_shapes.append(jax.ShapeDtypeStruct((nb, dil, seq // dil, GROUP_WIDTH), BF16))
            out_specs.append(pl.BlockSpec((1, dil, tm // dil, GROUP_WIDTH), lambda b, i: (b, 0, i, 0)))
    out_shapes.append(jax.ShapeDtypeStruct((nb, seq, D_MODEL), F32))
    out_specs.append(pl.BlockSpec((1, tm, D_MODEL), lambda b, i: (b, i, 0)))
    return pl.pallas_call(
        _qkv_kernel,
        grid=(nb, seq // tm),
        in_specs=[
            pl.BlockSpec((1, tm, D_MODEL), lambda b, i: (b, i, 0)),
            pl.BlockSpec((1, 6, D_MODEL), lambda b, i: (b, 0, 0)),
            pl.BlockSpec((1, D_MODEL), lambda b, i: (0, 0)),
            pl.BlockSpec((3, tm, LANES), lambda b, i: (0, 0, 0)),
            pl.BlockSpec((1, 3, LANES), lambda b, i: (i, 0, 0)),
            pl.BlockSpec((D_MODEL, n_cols), lambda b, i: (0, 0), pipeline_mode=pl.Buffered(1)),
        ],
        out_specs=out_specs,
        out_shape=out_shapes,
        scratch_shapes=[pltpu.VMEM((ROW_PARTS, 3 * (N_GROUPS - 1), GROUP_WIDTH // LANES,
                                    tm // ROW_PARTS, LANES), F32)],
        compiler_params=_params(2),
        name="qkv",
    )(x, mod, g1, rope_base, rope_tile, w_in)


def _attn_kernel(*refs):
    in_refs = refs[:7 * N_GROUPS]
    att_ref = refs[7 * N_GROUPS]
    (o_scr, lse_scr, kf_scr, vf_scr, st_scr, mt_scr, biast_scr,
     sr_scr, mr_scr, biasr_scr) = refs[7 * N_GROUPS + 1:]
    tm = att_ref.shape[1]
    i = pl.program_id(1)
    last = pl.num_programs(1) - 1

    n_pairs = GROUP_WIDTH // LANES
    lane = lax.broadcasted_iota(jnp.int32, (1, LANES), 1)
    head_a = lane < HEAD_DIM
    tk = ATT_K_BLOCK
    mask_forms = {}

    def build_masks():
        for (transposed, tq), bias_scr in mask_forms.items():
            if transposed:
                key = lax.broadcasted_iota(jnp.int32, (tk, tq), 0)
                qry = lax.broadcasted_iota(jnp.int32, (tk, tq), 1)
            else:
                qry = lax.broadcasted_iota(jnp.int32, (tq, tk), 0)
                key = lax.broadcasted_iota(jnp.int32, (tq, tk), 1)
            rel = key - BAND - qry
            band = (rel >= -BAND) & (rel <= BAND)
            after_start = key >= BAND
            before_end = key < tq + BAND
            for variant, valid in enumerate((band, band & after_start, band & before_end,
                                             band & after_start & before_end)):
                bias_scr[variant] = jnp.where(valid, 0.0, NEG).astype(F32)

    for g, dil in enumerate(DILS):
        tq = min(ATT_Q_BLOCK, tm // dil)
        mask_forms.setdefault((tq == LANES, tq), biast_scr if tq == LANES else biasr_scr)
    pl.when(jnp.logical_and(pl.program_id(0) == 0, i == 0))(build_masks)

    for g, dil in enumerate(DILS):
        q_ref, kp_ref, kc_ref, kn_ref, vp_ref, vc_ref, vn_ref = in_refs[7 * g:7 * g + 7]
        n = tm // dil
        tq = min(ATT_Q_BLOCK, n)
        nsub = n // tq
        per_body = ATT_BLOCKS_PER_BODY[g]
        n_iters = dil * nsub // per_body
        one_class = nsub >= per_body
        if one_class:
            assert dil == 1 and nsub % per_body == 0
        else:
            assert per_body % (2 * nsub) == 0 and dil % (per_body // nsub) == 0

        transposed = tq == LANES
        s_scr, m_scr, bias_scr = (st_scr, mt_scr, biast_scr) if transposed else (sr_scr, mr_scr, biasr_scr)

        def block_coords(it, j, nsub=nsub, per_body=per_body, one_class=one_class):
            if one_class:
                return 0, it * per_body + j, 0
            return it * (per_body // nsub) + j // nsub, j % nsub, (j // nsub) % 2

        def stage_kv(r, slot, n=n, tq=tq, kp_ref=kp_ref, kc_ref=kc_ref, kn_ref=kn_ref,
                     vp_ref=vp_ref, vc_ref=vc_ref, vn_ref=vn_ref):
            pad = tk - tq - 2 * BAND
            for full, before, cur, after in ((kf_scr, kp_ref, kc_ref, kn_ref),
                                             (vf_scr, vp_ref, vc_ref, vn_ref)):
                full[slot, 0:BAND] = before[0, r]
                full[slot, BAND:BAND + n] = cur[0, r]
                full[slot, BAND + n:2 * BAND + n] = after[0, r]
                if pad:
                    full[slot, 2 * BAND + n:2 * BAND + n + pad] = jnp.zeros((pad, GROUP_WIDTH), full.dtype)

        def edge_bias(sub, nsub=nsub, bias_scr=bias_scr):
            first = jnp.logical_and(i == 0, sub == 0)
            final = jnp.logical_and(i == last, sub == nsub - 1)
            return bias_scr[first.astype(jnp.int32) + 2 * final.astype(jnp.int32)]

        def scores(it, j, tq=tq, q_ref=q_ref, block_coords=block_coords, stage_kv=stage_kv,
                   one_class=one_class, edge_bias=edge_bias, transposed=transposed,
                   s_scr=s_scr, m_scr=m_scr):
            r, sub, kv_slot = block_coords(it, j)
            if not one_class and sub == 0:
                stage_kv(r, kv_slot)
            q0 = sub * tq if isinstance(sub, int) else pl.multiple_of(sub * tq, tq)
            bias = edge_bias(sub)
            for p in range(n_pairs):
                cols = slice(p * LANES, (p + 1) * LANES)
                qp = q_ref[0, r, pl.ds(q0, tq), cols]
                kp = kf_scr[kv_slot, pl.ds(q0, tk), cols]
                zeros = jnp.zeros_like(qp)
                q_heads = (jnp.where(head_a, qp, zeros), jnp.where(head_a, zeros, qp))
                if transposed:
                    s = lax.dot_general(kp, jnp.concatenate(q_heads, axis=0),
                                        (((1,), (1,)), ((), ())), preferred_element_type=F32)
                    s = jnp.concatenate([s[:, 0:tq] + bias, s[:, tq:2 * tq] + bias], axis=1)
                    s_scr[j % 2, p] = s
                    m_scr[j % 2, p] = jnp.broadcast_to(jnp.max(s, axis=0, keepdims=True),
                                                       (SUBLANES, 2 * tq))
                else:
                    s = lax.dot_general(jnp.concatenate(q_heads, axis=0), kp, (((1,), (1,)), ((), ())),
                                        preferred_element_type=F32)
                    s = s + jnp.concatenate([bias, bias], axis=0)
                    s_scr[j % 2, p] = s
                    m_scr[j % 2, p] = jnp.broadcast_to(jnp.max(s, axis=-1, keepdims=True),
                                                       (2 * tq, LANES))

        def finish(it, j, tq=tq, dil=dil, g=g, block_coords=block_coords, transposed=transposed,
                   s_scr=s_scr, m_scr=m_scr):
            r, sub, kv_slot = block_coords(it, j)
            q0 = sub * tq if isinstance(sub, int) else pl.multiple_of(sub * tq, tq)
            for p in range(n_pairs):
                cols = slice(p * LANES, (p + 1) * LANES)
                vp = vf_scr[kv_slot, pl.ds(q0, tk), cols]
                if transposed:
                    vt = vp.T
                    ones = jnp.ones((2 * SUBLANES, tk), BF16)
                    m = m_scr[j % 2, p]
                    e = jnp.exp2(s_scr[j % 2, p] - m[0:1]).astype(BF16)
                    tops, lses = [], []
                    for hh in range(2):
                        lhs = jnp.concatenate([vt[hh * HEAD_DIM:(hh + 1) * HEAD_DIM], ones], axis=0)
                        res = jnp.dot(lhs, e[:, hh * tq:(hh + 1) * tq], preferred_element_type=F32)
                        den = res[HEAD_DIM:HEAD_DIM + SUBLANES]
                        tops.append(res[0:HEAD_DIM] * jnp.tile(1.0 / den, (HEAD_DIM // SUBLANES, 1)))
                        lse = m[:, hh * tq:(hh + 1) * tq] + jnp.log2(den)
                        lses.append(jnp.tile(lse, (HEAD_DIM // SUBLANES, 1)))
                    o2 = jnp.concatenate(tops, axis=0).T
                    lse2 = jnp.concatenate(lses, axis=0).T
                else:
                    s = s_scr[j % 2, p]
                    m = m_scr[j % 2, p]
                    e = jnp.concatenate([jnp.exp2(s[:, t * LANES:(t + 1) * LANES] - m)
                                         for t in range(tk // LANES)], axis=1)
                    rhs = jnp.concatenate([vp, jnp.ones_like(vp)], axis=1)
                    res = jnp.dot(e.astype(BF16), rhs, preferred_element_type=F32)
                    den = jnp.where(head_a, res[0:tq, LANES:2 * LANES], res[tq:2 * tq, LANES:2 * LANES])
                    o2 = jnp.where(head_a, res[0:tq, 0:LANES], res[tq:2 * tq, 0:LANES]) / den
                    lse2 = jnp.where(head_a, m[0:tq], m[tq:2 * tq]) + jnp.log2(den)
                if dil == 1:
                    rows = pl.ds(q0, tq)
                else:
                    rows = pl.ds(r + dil * q0, tq, stride=dil)
                o_scr[g * n_pairs + p, rows, :] = o2
                lse_scr[g * n_pairs + p, rows, :] = lse2

        if one_class:
            stage_kv(0, 0)
        scores(0, 0)

        def body(it, carry, per_body=per_body, n_iters=n_iters, scores=scores, finish=finish):
            for j in range(per_body):
                if j + 1 < per_body:
                    scores(it, j + 1)
                else:
                    scores(jnp.minimum(it + 1, n_iters - 1), 0)
                finish(it, j)
            return carry

        lax.fori_loop(0, n_iters, body, 0)

    chunk = 128

    def merge_body(c, carry):
        rows = pl.ds(pl.multiple_of(c * chunk, chunk), chunk)
        for p in range(n_pairs):
            ls = [lse_scr[g * n_pairs + p, rows, :] for g in range(N_GROUPS)]
            mx = functools.reduce(jnp.maximum, ls)
            ws = [jnp.exp2(l - mx) for l in ls]
            num = sum(w * o_scr[g * n_pairs + p, rows, :] for g, w in enumerate(ws))
            att_ref[0, rows, p * LANES:(p + 1) * LANES] = (num / sum(ws)).astype(att_ref.dtype)
        return carry

    lax.fori_loop(0, tm // chunk, merge_body, 0)


def _attn(qkv):
    nb = qkv[0].shape[0]
    seq = qkv[0].shape[1] * qkv[0].shape[2]
    tm = ATT_TILE
    operands = []
    in_specs = []
    kv_rows = 0
    for g, dil in enumerate(DILS):
        q, k, v = qkv[3 * g:3 * g + 3]
        n = tm // dil
        nblk = n // BAND
        n_halo = seq // dil // BAND
        cur = pl.BlockSpec((1, dil, n, GROUP_WIDTH), lambda b, i: (b, 0, i, 0))
        prev = pl.BlockSpec((1, dil, BAND, GROUP_WIDTH),
                            lambda b, i, nblk=nblk: (b, 0, jnp.maximum(i * nblk - 1, 0), 0))
        nxt = pl.BlockSpec((1, dil, BAND, GROUP_WIDTH),
                           lambda b, i, nblk=nblk, n_halo=n_halo: (b, 0, jnp.minimum((i + 1) * nblk, n_halo - 1), 0))
        operands += [q, k, k, k, v, v, v]
        in_specs += [cur, prev, cur, nxt, prev, cur, nxt]
        kv_rows = max(kv_rows, n - min(ATT_Q_BLOCK, n) + ATT_K_BLOCK)
    merge_shape = (N_GROUPS * GROUP_WIDTH // LANES, tm, LANES)
    tq_small = min(ATT_Q_BLOCK, tm // max(DILS))
    scratch = [
        pltpu.VMEM(merge_shape, F32),
        pltpu.VMEM(merge_shape, F32),
        pltpu.VMEM((2, kv_rows, GROUP_WIDTH), BF16),
        pltpu.VMEM((2, kv_rows, GROUP_WIDTH), BF16),
        pltpu.VMEM((2, GROUP_WIDTH // LANES, ATT_K_BLOCK, 2 * ATT_Q_BLOCK), F32),
        pltpu.VMEM((2, GROUP_WIDTH // LANES, SUBLANES, 2 * ATT_Q_BLOCK), F32),
        pltpu.VMEM((4, ATT_K_BLOCK, ATT_Q_BLOCK), F32),
        pltpu.VMEM((2, GROUP_WIDTH // LANES, 2 * tq_small, ATT_K_BLOCK), F32),
        pltpu.VMEM((2, GROUP_WIDTH // LANES, 2 * tq_small, LANES), F32),
        pltpu.VMEM((4, tq_small, ATT_K_BLOCK), F32),
    ]
    return pl.pallas_call(
        _attn_kernel,
        grid=(nb, seq // tm),
        in_specs=in_specs,
        out_specs=pl.BlockSpec((1, tm, GROUP_WIDTH), lambda b, i: (b, i, 0)),
        out_shape=jax.ShapeDtypeStruct((nb, seq, GROUP_WIDTH), BF16),
        scratch_shapes=scratch,
        compiler_params=_params(2, "arbitrary"),
        name="attn",
    )(*operands)


def _mix_kernel(x_ref, att_ref, u_ref, up_ref, un_ref, mod_ref, g1_ref, cw_ref, cb_ref,
                w3_ref, wa_ref, wc_ref, wm_ref, o_ref):
    tm = x_ref.shape[1]
    i = pl.program_id(1)
    last = pl.num_programs(1) - 1
    mod = mod_ref[0]
    halo = up_ref.shape[1]
    cw = cw_ref[...]
    rp = tm // ROW_PARTS
    row_id = lax.broadcasted_iota(jnp.int32, (rp, 1), 0)

    for part in range(ROW_PARTS):
        rows = slice(part * rp, (part + 1) * rp)
        x = x_ref[0, rows]
        xb = _modulated_norm(x, g1_ref[...], mod[1:2], mod[0:1]).astype(BF16)

        u = u_ref[0, rows]
        if part == 0:
            u_before = jnp.where(i > 0, up_ref[0, halo - 1:halo, :], 0.0)
        else:
            u_before = u_ref[0, part * rp - 1:part * rp, :]
        if part == ROW_PARTS - 1:
            u_after = jnp.where(i < last, un_ref[0, 0:1, :], 0.0)
        else:
            u_after = u_ref[0, (part + 1) * rp:(part + 1) * rp + 1, :]
        u_m1 = jnp.where(row_id == 0, u_before, pltpu.roll(u, 1, 0))
        u_p1 = jnp.where(row_id == rp - 1, u_after, pltpu.roll(u, rp - 1, 0))
        conv = cb_ref[...] + u_m1 * cw[0:1] + u * cw[1:2] + u_p1 * cw[2:3]

        b_gate = jnp.dot(xb, w3_ref[:, 0:D_MODEL], preferred_element_type=F32)
        cv = jnp.dot((b_gate * conv).astype(BF16), wc_ref[...], preferred_element_type=F32)
        g_att = jnp.dot(xb, w3_ref[:, D_MODEL:2 * D_MODEL], preferred_element_type=F32)
        ap = jnp.dot(att_ref[0, rows], wa_ref[...], preferred_element_type=F32)
        merged = jax.nn.sigmoid(g_att) * ap
        g_conv = jnp.dot(xb, w3_ref[:, 2 * D_MODEL:3 * D_MODEL], preferred_element_type=F32)
        merged = merged + jax.nn.sigmoid(g_conv) * cv
        mix = jnp.dot(merged.astype(BF16), wm_ref[...], preferred_element_type=F32)
        o_ref[0, rows] = x + mod[2:3] * mix


def _mix(x, att, u, mod, g1, conv_w, conv_b, w_bgg, w_att, w_conv, w_mixo):
    nb, seq, _ = x.shape
    tm = QKV_TILE
    halo = 8
    nh = tm // halo
    return pl.pallas_call(
        _mix_kernel,
        grid=(nb, seq // tm),
        in_specs=[
            pl.BlockSpec((1, tm, D_MODEL), lambda b, i: (b, i, 0)),
            pl.BlockSpec((1, tm, GROUP_WIDTH), lambda b, i: (b, i, 0)),
            pl.BlockSpec((1, tm, D_MODEL), lambda b, i: (b, i, 0)),
            pl.BlockSpec((1, halo, D_MODEL), lambda b, i: (b, jnp.maximum(i * nh - 1, 0), 0)),
            pl.BlockSpec((1, halo, D_MODEL), lambda b, i: (b, jnp.minimum((i + 1) * nh, seq // halo - 1), 0)),
            pl.BlockSpec((1, 6, D_MODEL), lambda b, i: (b, 0, 0)),
            pl.BlockSpec((1, D_MODEL), lambda b, i: (0, 0)),
            pl.BlockSpec((CONV_WIDTH, D_MODEL), lambda b, i: (0, 0)),
            pl.BlockSpec((1, D_MODEL), lambda b, i: (0, 0)),
            _resident(w_bgg.shape),
            _resident(w_att.shape),
            _resident(w_conv.shape),
            _resident(w_mixo.shape),
        ],
        out_specs=pl.BlockSpec((1, tm, D_MODEL), lambda b, i: (b, i, 0)),
        out_shape=jax.ShapeDtypeStruct((nb, seq, D_MODEL), F32),
        compiler_params=_params(2),
        name="mix",
    )(x, att, u, u, u, mod, g1, conv_w, conv_b, w_bgg, w_att, w_conv, w_mixo)


def _mlp_kernel(x_ref, mod_ref, g2_ref, gf_ref, w1_ref, w2_ref, o_ref):
    mod = mod_ref[0]
    tm = x_ref.shape[1]
    rp = tm // ROW_PARTS
    ff_chunk = D_MODEL
    for part in range(ROW_PARTS):
        rows = slice(part * rp, (part + 1) * rp)
        x = x_ref[0, rows]
        xb = _modulated_norm(x, g2_ref[...], mod[4:5], mod[3:4]).astype(BF16)
        acc = jnp.zeros(x.shape, F32)
        for c in range(D_FF // ff_chunk):
            cols = slice(c * ff_chunk, (c + 1) * ff_chunk)
            h = jnp.dot(xb, w1_ref[:, cols], preferred_element_type=F32)
            h = jnp.square(jnp.maximum(h, 0.0)).astype(BF16)
            acc = acc + jnp.dot(h, w2_ref[cols, :], preferred_element_type=F32)
        x2 = x + mod[5:6] * acc
        r = lax.rsqrt(jnp.mean(x2 * x2, axis=-1, keepdims=True) + EPS)
        o_ref[0, rows] = (x2 * r) * gf_ref[...]


def _mlp(x1, mod, g2, gf, w1, w2):
    nb, seq, _ = x1.shape
    tm = QKV_TILE
    return pl.pallas_call(
        _mlp_kernel,
        grid=(nb, seq // tm),
        in_specs=[
            pl.BlockSpec((1, tm, D_MODEL), lambda b, i: (b, i, 0)),
            pl.BlockSpec((1, 6, D_MODEL), lambda b, i: (b, 0, 0)),
            pl.BlockSpec((1, D_MODEL), lambda b, i: (0, 0)),
            pl.BlockSpec((1, D_MODEL), lambda b, i: (0, 0)),
            _resident(w1.shape),
            _resident(w2.shape),
        ],
        out_specs=pl.BlockSpec((1, tm, D_MODEL), lambda b, i: (b, i, 0)),
        out_shape=jax.ShapeDtypeStruct((nb, seq, D_MODEL), F32),
        compiler_params=_params(2),
        name="mlp",
    )(x1, mod, g2, gf, w1, w2)


def _rope_tables(seq, tm):
    inv = 1.0 / (ROPE_THETA ** (np.arange(HALF_ROT, dtype=np.float64) / HALF_ROT))
    inv = np.tile(inv, LANES // HALF_ROT)[None, :]
    sign = np.where((np.arange(LANES) // HALF_ROT) % 2 == 0, -1.0, 1.0)[None, :]
    ang_b = np.arange(tm, dtype=np.float64)[:, None] * inv
    ang_a = (np.arange(seq // tm, dtype=np.float64) * tm)[:, None] * inv
    base = np.stack([np.cos(ang_b), np.sin(ang_b), sign * np.sin(ang_b)])
    tile = np.stack([np.cos(ang_a), np.sin(ang_a), sign * np.sin(ang_a)], axis=1)
    return jnp.asarray(base, F32), jnp.asarray(tile, F32)


def _encoder(x, mod, g1, g2, gf, conv_w, conv_b, w_in, w_bgg, w_att, w_conv, w_mixo, w1, w2):
    outs = _qkv(x, mod, g1, w_in)
    att = _attn(outs[:3 * N_GROUPS])
    u = outs[3 * N_GROUPS]
    x1 = _mix(x, att, u, mod, g1, conv_w, conv_b, w_bgg, w_att, w_conv, w_mixo)
    return _mlp(x1, mod, g2, gf, w1, w2)


def kernel(x_prompt, x_sample, c_prompt, c_sample, w_ada, b_ada, norm1_g, w_in, conv_w, conv_b,
           w_attn_out, w_conv_out, w_mix_out, norm2_g, w_mlp_in, w_mlp_out, final_norm_g):
    assert w_ada.shape[0] == 1, "single layer"
    nb_p, nb_s = x_prompt.shape[0], x_sample.shape[0]
    pad = -(nb_p + nb_s) % 8
    c_all = jnp.concatenate([c_prompt, c_sample, jnp.zeros((pad, D_MODEL), F32)], axis=0)
    mod = _ada(c_all, w_ada[0], b_ada[0][None, :])
    mod = mod.reshape(mod.shape[0], 6, D_MODEL)

    w_in_b = w_in[0].astype(BF16)
    w_bgg = w_in_b[:, 3 * ATT_WIDTH + 2 * D_MODEL:]
    shared = (norm1_g, norm2_g, final_norm_g[None, :], conv_w[0], conv_b, w_in_b, w_bgg,
              w_attn_out[0].astype(BF16), w_conv_out[0].astype(BF16), w_mix_out[0].astype(BF16),
              w_mlp_in[0].astype(BF16), w_mlp_out[0].astype(BF16))

    ys = []
    for x, m in ((x_prompt, mod[:nb_p]), (x_sample, mod[nb_p:nb_p + nb_s])):
        ys.append(_encoder(x, m, *shared))
    return tuple(ys)
```

```python
import functools

import jax
import jax.numpy as jnp
import numpy as np
from jax import lax
from jax.experimental import pallas as pl
from jax.experimental.pallas import tpu as pltpu

D_MODEL = 1024
HEAD_DIM = 64
HEADS_PER_GROUP = 8
GROUP_WIDTH = HEADS_PER_GROUP * HEAD_DIM
DILATION_GROUPS = ((128, 1), (512, 4), (2048, 16))
DILS = tuple(d for _, d in DILATION_GROUPS)
BAND = DILATION_GROUPS[0][0] // (2 * DILATION_GROUPS[0][1])
assert all(w // (2 * d) == BAND for w, d in DILATION_GROUPS)
N_GROUPS = len(DILS)
ATT_WIDTH = N_GROUPS * GROUP_WIDTH
CONV_WIDTH = 3
D_FF = 4 * D_MODEL
ROPE_THETA = 10000.0
EPS = 1e-6
NEG = -1e30

LANES = 128
SUBLANES = 8
HALF_ROT = HEAD_DIM // 2
QK_SCALE = HEAD_DIM ** -0.5 * np.log2(np.e)

QKV_TILE = 512
MLP_TILE = 1024
PART_ROWS = 256
ATT_TILE = 1024
ATT_Q_BLOCK = 2 * BAND
ATT_K_BLOCK = 4 * BAND
ATT_BLOCKS_PER_BODY = (4, 4, 8)
VMEM_LIMIT = 60 * 1024 * 1024

F32 = jnp.float32
BF16 = jnp.bfloat16


def _resident(shape):
    return pl.BlockSpec(shape, lambda *_: (0,) * len(shape), pipeline_mode=pl.Buffered(1))


def _params(n_axes, semantics="parallel"):
    return pltpu.CompilerParams(
        dimension_semantics=(semantics,) * n_axes, vmem_limit_bytes=VMEM_LIMIT)


def _modulated_norm(x, gain, scale, shift):
    r = lax.rsqrt(jnp.mean(x * x, axis=-1, keepdims=True) + EPS)
    return (x * r) * (gain * (1.0 + scale)) + shift


def _ada_kernel(c_ref, w_ref, b_ref, o_ref):
    c = c_ref[...]
    s = c * jax.nn.sigmoid(c)
    o_ref[...] = jnp.dot(s, w_ref[...], precision=lax.Precision.HIGHEST,
                         preferred_element_type=F32) + b_ref[...]


def _ada(c_all, w_ada, b_ada):
    rows, _ = c_all.shape
    n_out = w_ada.shape[1]
    bn = n_out // 4
    return pl.pallas_call(
        _ada_kernel,
        grid=(n_out // bn,),
        in_specs=[
            pl.BlockSpec((rows, D_MODEL), lambda j: (0, 0)),
            pl.BlockSpec((D_MODEL, bn), lambda j: (0, j)),
            pl.BlockSpec((1, bn), lambda j: (0, j)),
        ],
        out_specs=pl.BlockSpec((rows, bn), lambda j: (0, j)),
        out_shape=jax.ShapeDtypeStruct((rows, n_out), F32),
        compiler_params=_params(1),
        name="ada",
    )(c_all, w_ada, b_ada)


def _qkv_kernel(x_ref, mod_ref, g1_ref, rope_base_ref, rope_tile_ref, w_ref, *rest):
    qkv_refs = rest[:3 * N_GROUPS]
    u_ref = rest[3 * N_GROUPS]
    scr_ref = rest[3 * N_GROUPS + 1]
    tm = x_ref.shape[1]
    mod = mod_ref[0]
    ta = rope_tile_ref[0]
    ca, sa, ssa = ta[0:1], ta[1:2], ta[2:3]
    lane = lax.broadcasted_iota(jnp.int32, (1, LANES), 1)
    first_half = (lane // HALF_ROT) % 2 == 0

    rp = PART_ROWS
    for part in range(tm // rp):
        rows = slice(part * rp, (part + 1) * rp)
        xn = _modulated_norm(x_ref[0, rows], g1_ref[...], mod[1:2], mod[0:1])
        xb = xn.astype(BF16)

        lhs = {1: xb}
        for c in range(D_MODEL // LANES):
            scr_ref[part, c] = xn[:, c * LANES:(c + 1) * LANES]
        for dil in DILS[1:]:
            n = rp // dil
            lhs[dil] = jnp.concatenate(
                [jnp.concatenate([scr_ref[part, c, pl.ds(r, n, stride=dil), :] for r in range(dil)], axis=0)
                 for c in range(D_MODEL // LANES)], axis=1).astype(BF16)

        def emit(out_ref, dil, val, part=part, rp=rp):
            n = rp // dil
            for r in range(dil):
                out_ref[0, r, part * n:(part + 1) * n] = val[r * n:(r + 1) * n].astype(BF16)

        def rope(val, scale, g, rows=rows):
            cb, sb, ssb = rope_base_ref[g, 0, rows], rope_base_ref[g, 1, rows], rope_base_ref[g, 2, rows]
            cos = (ca * cb - sa * sb) * scale
            sin = (ssa * cb + ca * ssb) * scale
            parts = []
            for p in range(GROUP_WIDTH // LANES):
                xp = val[:, p * LANES:(p + 1) * LANES]
                partner = jnp.where(first_half, pltpu.roll(xp, LANES - HALF_ROT, 1),
                                    pltpu.roll(xp, HALF_ROT, 1))
                parts.append(xp * cos + partner * sin)
            return jnp.concatenate(parts, axis=1)

        for which in range(3):
            for g, dil in enumerate(DILS):
                col = which * ATT_WIDTH + g * GROUP_WIDTH
                val = jnp.dot(lhs[dil], w_ref[:, col:col + GROUP_WIDTH], preferred_element_type=F32)
                if which < 2:
                    val = rope(val, QK_SCALE if which == 0 else 1.0, g)
                emit(qkv_refs[3 * g + which], dil, val)

        col = 3 * ATT_WIDTH
        h = jnp.dot(xb, w_ref[:, col:col + D_MODEL], preferred_element_type=F32)
        cg = jnp.dot(xb, w_ref[:, col + D_MODEL:col + 2 * D_MODEL], preferred_element_type=F32)
        u_ref[0, rows] = cg * h


def _qkv(x, mod, g1, w_in):
    nb, seq, _ = x.shape
    tm = QKV_TILE
    n_cols = 3 * ATT_WIDTH + 2 * D_MODEL
    rope_base, rope_tile = _rope_tables(seq, tm)
    out_shapes = []
    out_specs = []
    for dil in DILS:
        for _ in range(3):
            out_shapes.append(jax.ShapeDtypeStruct((nb, dil, seq // dil, GROUP_WIDTH), BF16))
            out_specs.append(pl.BlockSpec((1, dil, tm // dil, GROUP_WIDTH), lambda b, i: (b, 0, i, 0)))
    out_shapes.append(jax.ShapeDtypeStruct((nb, seq, D_MODEL), F32))
    out_specs.append(pl.BlockSpec((1, tm, D_MODEL), lambda b, i: (b, i, 0)))
    return pl.pallas_call(
        _qkv_kernel,
        grid=(nb, seq // tm),
        in_specs=[
            pl.BlockSpec((1, tm, D_MODEL), lambda b, i: (b, i, 0)),
            pl.BlockSpec((1, 6, D_MODEL), lambda b, i: (b, 0, 0)),
            pl.BlockSpec((1, D_MODEL), lambda b, i: (0, 0)),
            _resident((N_GROUPS, 3, tm, LANES)),
            pl.BlockSpec((1, 3, LANES), lambda b, i: (i, 0, 0)),
            pl.BlockSpec((D_MODEL, n_cols), lambda b, i: (0, 0), pipeline_mode=pl.Buffered(1)),
        ],
        out_specs=out_specs,
        out_shape=out_shapes,
        scratch_shapes=[pltpu.VMEM((tm // PART_ROWS, D_MODEL // LANES, PART_ROWS, LANES), F32)],
        compiler_params=_params(2),
        name="qkv",
    )(x, mod, g1, rope_base, rope_tile, w_in)


def _attn_kernel(*refs):
    in_refs = refs[:7 * N_GROUPS]
    att_ref = refs[7 * N_GROUPS]
    (o_scr, lse_scr, kf_scr, vf_scr, st_scr, mt_scr, biast_scr,
     sr_scr, mr_scr, biasr_scr) = refs[7 * N_GROUPS + 1:]
    tm = att_ref.shape[1]
    i = pl.program_id(1)
    last = pl.num_programs(1) - 1

    n_pairs = GROUP_WIDTH // LANES
    lane = lax.broadcasted_iota(jnp.int32, (1, LANES), 1)
    head_a = lane < HEAD_DIM
    tk = ATT_K_BLOCK
    mask_forms = {}

    def build_masks():
        for (transposed, tq), bias_scr in mask_forms.items():
            if transposed:
                key = lax.broadcasted_iota(jnp.int32, (tk, tq), 0)
                qry = lax.broadcasted_iota(jnp.int32, (tk, tq), 1)
            else:
                qry = lax.broadcasted_iota(jnp.int32, (tq, tk), 0)
                key = lax.broadcasted_iota(jnp.int32, (tq, tk), 1)
            rel = key - BAND - qry
            band = (rel >= -BAND) & (rel <= BAND)
            after_start = key >= BAND
            before_end = key < tq + BAND
            for variant, valid in enumerate((band, band & after_start, band & before_end,
                                             band & after_start & before_end)):
                bias_scr[variant] = jnp.where(valid, 0.0, NEG).astype(F32)

    for g, dil in enumerate(DILS):
        tq = min(ATT_Q_BLOCK, tm // dil)
        mask_forms.setdefault((tq == LANES, tq), biast_scr if tq == LANES else biasr_scr)
    pl.when(jnp.logical_and(pl.program_id(0) == 0, i == 0))(build_masks)

    for g, dil in enumerate(DILS):
        q_ref, kp_ref, kc_ref, kn_ref, vp_ref, vc_ref, vn_ref = in_refs[7 * g:7 * g + 7]
        n = tm // dil
        tq = min(ATT_Q_BLOCK, n)
        nsub = n // tq
        per_body = ATT_BLOCKS_PER_BODY[g]
        n_iters = dil * nsub // per_body
        one_class = nsub >= per_body
        if one_class:
            assert dil == 1 and nsub % per_body == 0
        else:
            assert per_body % (2 * nsub) == 0 and dil % (per_body // nsub) == 0

        transposed = tq == LANES
        s_scr, m_scr, bias_scr = (st_scr, mt_scr, biast_scr) if transposed else (sr_scr, mr_scr, biasr_scr)

        def block_coords(it, j, nsub=nsub, per_body=per_body, one_class=one_class):
            if one_class:
                return 0, it * per_body + j, 0
            return it * (per_body // nsub) + j // nsub, j % nsub, (j // nsub) % 2

        def stage_kv(r, slot, n=n, tq=tq, kp_ref=kp_ref, kc_ref=kc_ref, kn_ref=kn_ref,
                     vp_ref=vp_ref, vc_ref=vc_ref, vn_ref=vn_ref):
            pad = tk - tq - 2 * BAND
            for full, before, cur, after in ((kf_scr, kp_ref, kc_ref, kn_ref),
                                             (vf_scr, vp_ref, vc_ref, vn_ref)):
                full[slot, 0:BAND] = before[0, r]
                full[slot, BAND:BAND + n] = cur[0, r]
                full[slot, BAND + n:2 * BAND + n] = after[0, r]
                if pad:
                    full[slot, 2 * BAND + n:2 * BAND + n + pad] = jnp.zeros((pad, GROUP_WIDTH), full.dtype)

        def edge_bias(sub, nsub=nsub, bias_scr=bias_scr):
            first = jnp.logical_and(i == 0, sub == 0)
            final = jnp.logical_and(i == last, sub == nsub - 1)
            return bias_scr[first.astype(jnp.int32) + 2 * final.astype(jnp.int32)]

        def scores(it, j, tq=tq, q_ref=q_ref, block_coords=block_coords, stage_kv=stage_kv,
                   one_class=one_class, edge_bias=edge_bias, transposed=transposed,
                   s_scr=s_scr, m_scr=m_scr):
            r, sub, kv_slot = block_coords(it, j)
            if not one_class and sub == 0:
                stage_kv(r, kv_slot)
            q0 = sub * tq if isinstance(sub, int) else pl.multiple_of(sub * tq, tq)
            bias = edge_bias(sub)
            for p in range(n_pairs):
                cols = slice(p * LANES, (p + 1) * LANES)
                qp = q_ref[0, r, pl.ds(q0, tq), cols]
                kp = kf_scr[kv_slot, pl.ds(q0, tk), cols]
                zeros = jnp.zeros_like(qp)
                q_heads = (jnp.where(head_a, qp, zeros), jnp.where(head_a, zeros, qp))
                if transposed:
                    s = lax.dot_general(kp, jnp.concatenate(q_heads, axis=0),
                                        (((1,), (1,)), ((), ())), preferred_element_type=F32)
                    s = jnp.concatenate([s[:, 0:tq] + bias, s[:, tq:2 * tq] + bias], axis=1)
                    s_scr[j % 2, p] = s
                    m_scr[j % 2, p] = jnp.broadcast_to(jnp.max(s, axis=0, keepdims=True),
                                                       (SUBLANES, 2 * tq))
                else:
                    s = lax.dot_general(jnp.concatenate(q_heads, axis=0), kp, (((1,), (1,)), ((), ())),
                                        preferred_element_type=F32)
                    s = s + jnp.concatenate([bias, bias], axis=0)
                    s_scr[j % 2, p] = s
                    m_scr[j % 2, p] = jnp.broadcast_to(jnp.max(s, axis=-1, keepdims=True),
                                                       (2 * tq, LANES))

        def finish(it, j, tq=tq, dil=dil, g=g, block_coords=block_coords, transposed=transposed,
                   s_scr=s_scr, m_scr=m_scr):
            r, sub, kv_slot = block_coords(it, j)
            q0 = sub * tq if isinstance(sub, int) else pl.multiple_of(sub * tq, tq)
            for p in range(n_pairs):
                cols = slice(p * LANES, (p + 1) * LANES)
                vp = vf_scr[kv_slot, pl.ds(q0, tk), cols]
                if transposed:
                    vt = vp.T
                    ones = jnp.ones((2 * SUBLANES, tk), BF16)
                    m = m_scr[j % 2, p]
                    e = jnp.exp2(s_scr[j % 2, p] - m[0:1]).astype(BF16)
                    tops, lses = [], []
                    for hh in range(2):
                        lhs = jnp.concatenate([vt[hh * HEAD_DIM:(hh + 1) * HEAD_DIM], ones], axis=0)
                        res = jnp.dot(lhs, e[:, hh * tq:(hh + 1) * tq], preferred_element_type=F32)
                        den = res[HEAD_DIM:HEAD_DIM + SUBLANES]
                        tops.append(res[0:HEAD_DIM] * jnp.tile(1.0 / den, (HEAD_DIM // SUBLANES, 1)))
                        lse = m[:, hh * tq:(hh + 1) * tq] + jnp.log2(den)
                        lses.append(jnp.tile(lse, (HEAD_DIM // SUBLANES, 1)))
                    o2 = jnp.concatenate(tops, axis=0).T
                    lse2 = jnp.concatenate(lses, axis=0).T
                else:
                    s = s_scr[j % 2, p]
                    m = m_scr[j % 2, p]
                    e = jnp.concatenate([jnp.exp2(s[:, t * LANES:(t + 1) * LANES] - m)
                                         for t in range(tk // LANES)], axis=1)
                    rhs = jnp.concatenate([vp, jnp.ones_like(vp)], axis=1)
                    res = jnp.dot(e.astype(BF16), rhs, preferred_element_type=F32)
                    den = jnp.where(head_a, res[0:tq, LANES:2 * LANES], res[tq:2 * tq, LANES:2 * LANES])
                    o2 = jnp.where(head_a, res[0:tq, 0:LANES], res[tq:2 * tq, 0:LANES]) / den
                    lse2 = jnp.where(head_a, m[0:tq], m[tq:2 * tq]) + jnp.log2(den)
                if dil == 1:
                    rows = pl.ds(q0, tq)
                else:
                    rows = pl.ds(r + dil * q0, tq, stride=dil)
                o_scr[g * n_pairs + p, rows, :] = o2
                lse_scr[g * n_pairs + p, rows, :] = lse2

        if one_class:
            stage_kv(0, 0)
        scores(0, 0)

        def body(it, carry, per_body=per_body, n_iters=n_iters, scores=scores, finish=finish):
            for j in range(per_body):
                if j + 1 < per_body:
                    scores(it, j + 1)
                else:
                    scores(jnp.minimum(it + 1, n_iters - 1), 0)
                finish(it, j)
            return carry

        lax.fori_loop(0, n_iters, body, 0)

    chunk = 128

    def merge_body(c, carry):
        rows = pl.ds(pl.multiple_of(c * chunk, chunk), chunk)
        for p in range(n_pairs):
            ls = [lse_scr[g * n_pairs + p, rows, :] for g in range(N_GROUPS)]
            mx = functools.reduce(jnp.maximum, ls)
            ws = [jnp.exp2(l - mx) for l in ls]
            num = sum(w * o_scr[g * n_pairs + p, rows, :] for g, w in enumerate(ws))
            att_ref[0, rows, p * LANES:(p + 1) * LANES] = (num / sum(ws)).astype(att_ref.dtype)
        return carry

    lax.fori_loop(0, tm // chunk, merge_body, 0)


def _attn(qkv):
    nb = qkv[0].shape[0]
    seq = qkv[0].shape[1] * qkv[0].shape[2]
    tm = ATT_TILE
    operands = []
    in_specs = []
    kv_rows = 0
    for g, dil in enumerate(DILS):
        q, k, v = qkv[3 * g:3 * g + 3]
        n = tm // dil
        nblk = n // BAND
        n_halo = seq // dil // BAND
        cur = pl.BlockSpec((1, dil, n, GROUP_WIDTH), lambda b, i: (b, 0, i, 0))
        prev = pl.BlockSpec((1, dil, BAND, GROUP_WIDTH),
                            lambda b, i, nblk=nblk: (b, 0, jnp.maximum(i * nblk - 1, 0), 0))
        nxt = pl.BlockSpec((1, dil, BAND, GROUP_WIDTH),
                           lambda b, i, nblk=nblk, n_halo=n_halo: (b, 0, jnp.minimum((i + 1) * nblk, n_halo - 1), 0))
        operands += [q, k, k, k, v, v, v]
        in_specs += [cur, prev, cur, nxt, prev, cur, nxt]
        kv_rows = max(kv_rows, n - min(ATT_Q_BLOCK, n) + ATT_K_BLOCK)
    merge_shape = (N_GROUPS * GROUP_WIDTH // LANES, tm, LANES)
    tq_small = min(ATT_Q_BLOCK, tm // max(DILS))
    scratch = [
        pltpu.VMEM(merge_shape, F32),
        pltpu.VMEM(merge_shape, F32),
        pltpu.VMEM((2, kv_rows, GROUP_WIDTH), BF16),
        pltpu.VMEM((2, kv_rows, GROUP_WIDTH), BF16),
        pltpu.VMEM((2, GROUP_WIDTH // LANES, ATT_K_BLOCK, 2 * ATT_Q_BLOCK), F32),
        pltpu.VMEM((2, GROUP_WIDTH // LANES, SUBLANES, 2 * ATT_Q_BLOCK), F32),
        pltpu.VMEM((4, ATT_K_BLOCK, ATT_Q_BLOCK), F32),
        pltpu.VMEM((2, GROUP_WIDTH // LANES, 2 * tq_small, ATT_K_BLOCK), F32),
        pltpu.VMEM((2, GROUP_WIDTH // LANES, 2 * tq_small, LANES), F32),
        pltpu.VMEM((4, tq_small, ATT_K_BLOCK), F32),
    ]
    return pl.pallas_call(
        _attn_kernel,
        grid=(nb, seq // tm),
        in_specs=in_specs,
        out_specs=pl.BlockSpec((1, tm, GROUP_WIDTH), lambda b, i: (b, i, 0)),
        out_shape=jax.ShapeDtypeStruct((nb, seq, GROUP_WIDTH), BF16),
        scratch_shapes=scratch,
        compiler_params=_params(2, "arbitrary"),
        name="attn",
    )(*operands)


def _mix_kernel(x_ref, att_ref, u_ref, up_ref, un_ref, mod_ref, g1_ref, cw_ref, cb_ref,
                w3_ref, wa_ref, wc_ref, wm_ref, o_ref):
    tm = x_ref.shape[1]
    i = pl.program_id(1)
    last = pl.num_programs(1) - 1
    mod = mod_ref[0]
    halo = up_ref.shape[1]
    cw = cw_ref[...]
    rp = PART_ROWS
    n_parts = tm // rp
    row_id = lax.broadcasted_iota(jnp.int32, (rp, 1), 0)

    for part in range(n_parts):
        rows = slice(part * rp, (part + 1) * rp)
        x = x_ref[0, rows]
        xb = _modulated_norm(x, g1_ref[...], mod[1:2], mod[0:1]).astype(BF16)

        u = u_ref[0, rows]
        if part == 0:
            u_before = jnp.where(i > 0, up_ref[0, halo - 1:halo, :], 0.0)
        else:
            u_before = u_ref[0, part * rp - 1:part * rp, :]
        if part == n_parts - 1:
            u_after = jnp.where(i < last, un_ref[0, 0:1, :], 0.0)
        else:
            u_after = u_ref[0, (part + 1) * rp:(part + 1) * rp + 1, :]
        u_m1 = jnp.where(row_id == 0, u_before, pltpu.roll(u, 1, 0))
        u_p1 = jnp.where(row_id == rp - 1, u_after, pltpu.roll(u, rp - 1, 0))
        conv = cb_ref[...] + u_m1 * cw[0:1] + u * cw[1:2] + u_p1 * cw[2:3]

        b_gate = jnp.dot(xb, w3_ref[:, 0:D_MODEL], preferred_element_type=F32)
        cv = jnp.dot((b_gate * conv).astype(BF16), wc_ref[...], preferred_element_type=F32)
        g_att = jnp.dot(xb, w3_ref[:, D_MODEL:2 * D_MODEL], preferred_element_type=F32)
        ap = jnp.dot(att_ref[0, rows], wa_ref[...], preferred_element_type=F32)
        merged = jax.nn.sigmoid(g_att) * ap
        g_conv = jnp.dot(xb, w3_ref[:, 2 * D_MODEL:3 * D_MODEL], preferred_element_type=F32)
        merged = merged + jax.nn.sigmoid(g_conv) * cv
        mix = jnp.dot(merged.astype(BF16), wm_ref[...], preferred_element_type=F32)
        o_ref[0, rows] = x + mod[2:3] * mix


def _mix(x, att, u, mod, g1, conv_w, conv_b, w_bgg, w_att, w_conv, w_mixo):
    nb, seq, _ = x.shape
    tm = QKV_TILE
    halo = 8
    nh = tm // halo
    return pl.pallas_call(
        _mix_kernel,
        grid=(nb, seq // tm),
        in_specs=[
            pl.BlockSpec((1, tm, D_MODEL), lambda b, i: (b, i, 0)),
            pl.BlockSpec((1, tm, GROUP_WIDTH), lambda b, i: (b, i, 0)),
            pl.BlockSpec((1, tm, D_MODEL), lambda b, i: (b, i, 0)),
            pl.BlockSpec((1, halo, D_MODEL), lambda b, i: (b, jnp.maximum(i * nh - 1, 0), 0)),
            pl.BlockSpec((1, halo, D_MODEL), lambda b, i: (b, jnp.minimum((i + 1) * nh, seq // halo - 1), 0)),
            pl.BlockSpec((1, 6, D_MODEL), lambda b, i: (b, 0, 0)),
            pl.BlockSpec((1, D_MODEL), lambda b, i: (0, 0)),
            pl.BlockSpec((CONV_WIDTH, D_MODEL), lambda b, i: (0, 0)),
            pl.BlockSpec((1, D_MODEL), lambda b, i: (0, 0)),
            _resident(w_bgg.shape),
            _resident(w_att.shape),
            _resident(w_conv.shape),
            _resident(w_mixo.shape),
        ],
        out_specs=pl.BlockSpec((1, tm, D_MODEL), lambda b, i: (b, i, 0)),
        out_shape=jax.ShapeDtypeStruct((nb, seq, D_MODEL), F32),
        compiler_params=_params(2),
        name="mix",
    )(x, att, u, u, u, mod, g1, conv_w, conv_b, w_bgg, w_att, w_conv, w_mixo)


def _mlp_kernel(x_ref, mod_ref, g2_ref, gf_ref, w1_ref, w2_ref, o_ref):
    mod = mod_ref[0]
    tm = x_ref.shape[1]
    rp = PART_ROWS
    ff_chunk = D_MODEL
    for part in range(tm // rp):
        rows = slice(part * rp, (part + 1) * rp)
        x = x_ref[0, rows]
        xb = _modulated_norm(x, g2_ref[...], mod[4:5], mod[3:4]).astype(BF16)
        acc = jnp.zeros(x.shape, F32)
        for c in range(D_FF // ff_chunk):
            cols = slice(c * ff_chunk, (c + 1) * ff_chunk)
            h = jnp.dot(xb, w1_ref[:, cols], preferred_element_type=F32)
            h = jnp.square(jnp.maximum(h, 0.0)).astype(BF16)
            acc = acc + jnp.dot(h, w2_ref[cols, :], preferred_element_type=F32)
        x2 = x + mod[5:6] * acc
        r = lax.rsqrt(jnp.mean(x2 * x2, axis=-1, keepdims=True) + EPS)
        o_ref[0, rows] = (x2 * r) * gf_ref[...]


def _mlp(x1, mod, g2, gf, w1, w2):
    nb, seq, _ = x1.shape
    tm = MLP_TILE
    return pl.pallas_call(
        _mlp_kernel,
        grid=(nb, seq // tm),
        in_specs=[
            pl.BlockSpec((1, tm, D_MODEL), lambda b, i: (b, i, 0)),
            pl.BlockSpec((1, 6, D_MODEL), lambda b, i: (b, 0, 0)),
            pl.BlockSpec((1, D_MODEL), lambda b, i: (0, 0)),
            pl.BlockSpec((1, D_MODEL), lambda b, i: (0, 0)),
            _resident(w1.shape),
            _resident(w2.shape),
        ],
        out_specs=pl.BlockSpec((1, tm, D_MODEL), lambda b, i: (b, i, 0)),
        out_shape=jax.ShapeDtypeStruct((nb, seq, D_MODEL), F32),
        compiler_params=_params(2),
        name="mlp",
    )(x1, mod, g2, gf, w1, w2)


def _rope_tables(seq, tm):
    inv = 1.0 / (ROPE_THETA ** (np.arange(HALF_ROT, dtype=np.float64) / HALF_ROT))
    inv = np.tile(inv, LANES // HALF_ROT)[None, :]
    sign = np.where((np.arange(LANES) // HALF_ROT) % 2 == 0, -1.0, 1.0)[None, :]
    offsets = []
    for dil in DILS:
        o = np.arange(tm).reshape(tm // PART_ROWS, PART_ROWS // dil, dil)
        offsets.append(o.transpose(0, 2, 1).reshape(tm))
    ang_b = np.stack(offsets).astype(np.float64)[:, :, None] * inv[None]
    ang_a = (np.arange(seq // tm, dtype=np.float64) * tm)[:, None] * inv
    base = np.stack([np.cos(ang_b), np.sin(ang_b), sign * np.sin(ang_b)], axis=1)
    tile = np.stack([np.cos(ang_a), np.sin(ang_a), sign * np.sin(ang_a)], axis=1)
    return jnp.asarray(base, F32), jnp.asarray(tile, F32)


def _encoder(x, mod, g1, g2, gf, conv_w, conv_b, w_in, w_bgg, w_att, w_conv, w_mixo, w1, w2):
    outs = _qkv(x, mod, g1, w_in)
    att = _attn(outs[:3 * N_GROUPS])
    u = outs[3 * N_GROUPS]
    x1 = _mix(x, att, u, mod, g1, conv_w, conv_b, w_bgg, w_att, w_conv, w_mixo)
    return _mlp(x1, mod, g2, gf, w1, w2)


def kernel(x_prompt, x_sample, c_prompt, c_sample, w_ada, b_ada, norm1_g, w_in, conv_w, conv_b,
           w_attn_out, w_conv_out, w_mix_out, norm2_g, w_mlp_in, w_mlp_out, final_norm_g):
    assert w_ada.shape[0] == 1, "single layer"
    nb_p, nb_s = x_prompt.shape[0], x_sample.shape[0]
    pad = -(nb_p + nb_s) % 8
    c_all = jnp.concatenate([c_prompt, c_sample, jnp.zeros((pad, D_MODEL), F32)], axis=0)
    mod = _ada(c_all, w_ada[0], b_ada[0][None, :])
    mod = mod.reshape(mod.shape[0], 6, D_MODEL)

    w_in_b = w_in[0].astype(BF16)
    w_bgg = w_in_b[:, 3 * ATT_WIDTH + 2 * D_MODEL:]
    shared = (norm1_g, norm2_g, final_norm_g[None, :], conv_w[0], conv_b, w_in_b, w_bgg,
              w_attn_out[0].astype(BF16), w_conv_out[0].astype(BF16), w_mix_out[0].astype(BF16),
              w_mlp_in[0].astype(BF16), w_mlp_out[0].astype(BF16))

    ys = []
    for x, m in ((x_prompt, mod[:nb_p]), (x_sample, mod[nb_p:nb_p + nb_s])):
        ys.append(_encoder(x, m, *shared))
    return tuple(ys)
```

```python
import functools

import jax
import jax.numpy as jnp
import numpy as np
from jax import lax
from jax.experimental import pallas as pl
from jax.experimental.pallas import tpu as pltpu

D_MODEL = 1024
HEAD_DIM = 64
HEADS_PER_GROUP = 8
GROUP_WIDTH = HEADS_PER_GROUP * HEAD_DIM
DILATION_GROUPS = ((128, 1), (512, 4), (2048, 16))
DILS = tuple(d for _, d in DILATION_GROUPS)
BAND = DILATION_GROUPS[0][0] // (2 * DILATION_GROUPS[0][1])
assert all(w // (2 * d) == BAND for w, d in DILATION_GROUPS)
N_GROUPS = len(DILS)
ATT_WIDTH = N_GROUPS * GROUP_WIDTH
CONV_WIDTH = 3
D_FF = 4 * D_MODEL
ROPE_THETA = 10000.0
EPS = 1e-6
NEG = -1e30

LANES = 128
SUBLANES = 8
HALF_ROT = HEAD_DIM // 2
QK_SCALE = HEAD_DIM ** -0.5 * np.log2(np.e)

QKV_TILE = 512
MLP_TILE = 1024
PART_ROWS = 256
ATT_TILE = 1024
ATT_Q_BLOCK = 2 * BAND
ATT_K_BLOCK = 4 * BAND
MAX_STRIDE = 4
ATT_BLOCKS_PER_BODY = (4, 4, 8)
VMEM_LIMIT = 60 * 1024 * 1024

F32 = jnp.float32
BF16 = jnp.bfloat16


def _resident(shape):
    return pl.BlockSpec(shape, lambda *_: (0,) * len(shape), pipeline_mode=pl.Buffered(1))


def _params(n_axes, semantics="parallel"):
    return pltpu.CompilerParams(
        dimension_semantics=(semantics,) * n_axes, vmem_limit_bytes=VMEM_LIMIT)


def _modulated_norm(x, gain, scale, shift):
    r = lax.rsqrt(jnp.mean(x * x, axis=-1, keepdims=True) + EPS)
    return (x * r) * (gain * (1.0 + scale)) + shift


def _ada_kernel(c_ref, w_ref, b_ref, o_ref):
    c = c_ref[...]
    s = c * jax.nn.sigmoid(c)
    o_ref[...] = jnp.dot(s, w_ref[...], precision=lax.Precision.HIGHEST,
                         preferred_element_type=F32) + b_ref[...]


def _ada(c_all, w_ada, b_ada):
    rows, _ = c_all.shape
    n_out = w_ada.shape[1]
    bn = n_out // 4
    return pl.pallas_call(
        _ada_kernel,
        grid=(n_out // bn,),
        in_specs=[
            pl.BlockSpec((rows, D_MODEL), lambda j: (0, 0)),
            pl.BlockSpec((D_MODEL, bn), lambda j: (0, j)),
            pl.BlockSpec((1, bn), lambda j: (0, j)),
        ],
        out_specs=pl.BlockSpec((rows, bn), lambda j: (0, j)),
        out_shape=jax.ShapeDtypeStruct((rows, n_out), F32),
        compiler_params=_params(1),
        name="ada",
    )(c_all, w_ada, b_ada)


def _qkv_kernel(x_ref, mod_ref, g1_ref, rope_base_ref, rope_tile_ref, w_ref, *rest):
    qkv_refs = rest[:3 * N_GROUPS]
    u_ref = rest[3 * N_GROUPS]
    scr_ref = rest[3 * N_GROUPS + 1]
    tm = x_ref.shape[1]
    mod = mod_ref[0]
    ta = rope_tile_ref[0]
    ca, sa, ssa = ta[0:1], ta[1:2], ta[2:3]
    lane = lax.broadcasted_iota(jnp.int32, (1, LANES), 1)
    first_half = (lane // HALF_ROT) % 2 == 0

    rp = PART_ROWS
    for part in range(tm // rp):
        rows = slice(part * rp, (part + 1) * rp)
        xn = _modulated_norm(x_ref[0, rows], g1_ref[...], mod[1:2], mod[0:1])
        xb = xn.astype(BF16)

        lhs = {1: xb}
        for c in range(D_MODEL // LANES):
            scr_ref[part, c] = xn[:, c * LANES:(c + 1) * LANES]
        for dil in DILS[1:]:
            n = rp // dil
            lhs[dil] = jnp.concatenate(
                [jnp.concatenate([scr_ref[part, c, pl.ds(r, n, stride=dil), :] for r in range(dil)], axis=0)
                 for c in range(D_MODEL // LANES)], axis=1).astype(BF16)

        def emit(out_ref, dil, val, part=part, rp=rp):
            n = rp // dil
            for r in range(dil):
                out_ref[0, r, part * n:(part + 1) * n] = val[r * n:(r + 1) * n].astype(BF16)

        def rope(val, scale, g, rows=rows):
            cb, sb, ssb = rope_base_ref[g, 0, rows], rope_base_ref[g, 1, rows], rope_base_ref[g, 2, rows]
            cos = (ca * cb - sa * sb) * scale
            sin = (ssa * cb + ca * ssb) * scale
            parts = []
            for p in range(GROUP_WIDTH // LANES):
                xp = val[:, p * LANES:(p + 1) * LANES]
                partner = jnp.where(first_half, pltpu.roll(xp, LANES - HALF_ROT, 1),
                                    pltpu.roll(xp, HALF_ROT, 1))
                parts.append(xp * cos + partner * sin)
            return jnp.concatenate(parts, axis=1)

        for which in range(3):
            for g, dil in enumerate(DILS):
                col = which * ATT_WIDTH + g * GROUP_WIDTH
                val = jnp.dot(lhs[dil], w_ref[:, col:col + GROUP_WIDTH], preferred_element_type=F32)
                if which < 2:
                    val = rope(val, QK_SCALE if which == 0 else 1.0, g)
                emit(qkv_refs[3 * g + which], dil, val)

        col = 3 * ATT_WIDTH
        h = jnp.dot(xb, w_ref[:, col:col + D_MODEL], preferred_element_type=F32)
        cg = jnp.dot(xb, w_ref[:, col + D_MODEL:col + 2 * D_MODEL], preferred_element_type=F32)
        u_ref[0, rows] = cg * h


def _qkv(x, mod, g1, w_in):
    nb, seq, _ = x.shape
    tm = QKV_TILE
    n_cols = 3 * ATT_WIDTH + 2 * D_MODEL
    rope_base, rope_tile = _rope_tables(seq, tm)
    out_shapes = []
    out_specs = []
    for dil in DILS:
        for _ in range(3):
            out_shapes.append(jax.ShapeDtypeStruct((nb, dil, seq // dil, GROUP_WIDTH), BF16))
            out_specs.append(pl.BlockSpec((1, dil, tm // dil, GROUP_WIDTH), lambda b, i: (b, 0, i, 0)))
    out_shapes.append(jax.ShapeDtypeStruct((nb, seq, D_MODEL), F32))
    out_specs.append(pl.BlockSpec((1, tm, D_MODEL), lambda b, i: (b, i, 0)))
    return pl.pallas_call(
        _qkv_kernel,
        grid=(nb, seq // tm),
        in_specs=[
            pl.BlockSpec((1, tm, D_MODEL), lambda b, i: (b, i, 0)),
            pl.BlockSpec((1, 6, D_MODEL), lambda b, i: (b, 0, 0)),
            pl.BlockSpec((1, D_MODEL), lambda b, i: (0, 0)),
            _resident((N_GROUPS, 3, tm, LANES)),
            pl.BlockSpec((1, 3, LANES), lambda b, i: (i, 0, 0)),
            pl.BlockSpec((D_MODEL, n_cols), lambda b, i: (0, 0), pipeline_mode=pl.Buffered(1)),
        ],
        out_specs=out_specs,
        out_shape=out_shapes,
        scratch_shapes=[pltpu.VMEM((tm // PART_ROWS, D_MODEL // LANES, PART_ROWS, LANES), F32)],
        compiler_params=_params(2),
        name="qkv",
    )(x, mod, g1, rope_base, rope_tile, w_in)


def _attn_kernel(*refs):
    in_refs = refs[:7 * N_GROUPS]
    att_ref = refs[7 * N_GROUPS]
    (o_scr, lse_scr, kf_scr, vf_scr, st_scr, mt_scr, biast_scr,
     sr_scr, mr_scr, biasr_scr) = refs[7 * N_GROUPS + 1:]
    tm = att_ref.shape[1]
    i = pl.program_id(1)
    last = pl.num_programs(1) - 1

    n_pairs = GROUP_WIDTH // LANES
    lane = lax.broadcasted_iota(jnp.int32, (1, LANES), 1)
    head_a = lane < HEAD_DIM
    tk = ATT_K_BLOCK
    mask_forms = {}

    def build_masks():
        for (transposed, tq), bias_scr in mask_forms.items():
            if transposed:
                key = lax.broadcasted_iota(jnp.int32, (tk, tq), 0)
                qry = lax.broadcasted_iota(jnp.int32, (tk, tq), 1)
            else:
                qry = lax.broadcasted_iota(jnp.int32, (tq, tk), 0)
                key = lax.broadcasted_iota(jnp.int32, (tq, tk), 1)
            rel = key - BAND - qry
            band = (rel >= -BAND) & (rel <= BAND)
            after_start = key >= BAND
            before_end = key < tq + BAND
            for variant, valid in enumerate((band, band & after_start, band & before_end,
                                             band & after_start & before_end)):
                bias_scr[variant] = jnp.where(valid, 0.0, NEG).astype(F32)

    for g, dil in enumerate(DILS):
        tq = min(ATT_Q_BLOCK, tm // dil)
        mask_forms.setdefault((tq == LANES, tq), biast_scr if tq == LANES else biasr_scr)
    pl.when(jnp.logical_and(pl.program_id(0) == 0, i == 0))(build_masks)

    for g in sorted(range(N_GROUPS), key=lambda g: -DILS[g]):
        dil = DILS[g]
        q_ref, kp_ref, kc_ref, kn_ref, vp_ref, vc_ref, vn_ref = in_refs[7 * g:7 * g + 7]
        two_hops = dil > MAX_STRIDE
        if two_hops:
            assert dil % MAX_STRIDE == 0 and dil // MAX_STRIDE <= MAX_STRIDE and MAX_STRIDE in DILS
        store_g = DILS.index(MAX_STRIDE) if two_hops else g
        n = tm // dil
        tq = min(ATT_Q_BLOCK, n)
        nsub = n // tq
        per_body = ATT_BLOCKS_PER_BODY[g]
        n_iters = dil * nsub // per_body
        one_class = nsub >= per_body
        if one_class:
            assert dil == 1 and nsub % per_body == 0
        else:
            assert per_body % (2 * nsub) == 0 and dil % (per_body // nsub) == 0

        transposed = tq == LANES
        s_scr, m_scr, bias_scr = (st_scr, mt_scr, biast_scr) if transposed else (sr_scr, mr_scr, biasr_scr)

        def block_coords(it, j, nsub=nsub, per_body=per_body, one_class=one_class):
            if one_class:
                return 0, it * per_body + j, 0
            return it * (per_body // nsub) + j // nsub, j % nsub, (j // nsub) % 2

        def stage_kv(r, slot, n=n, tq=tq, kp_ref=kp_ref, kc_ref=kc_ref, kn_ref=kn_ref,
                     vp_ref=vp_ref, vc_ref=vc_ref, vn_ref=vn_ref):
            pad = tk - tq - 2 * BAND
            for full, before, cur, after in ((kf_scr, kp_ref, kc_ref, kn_ref),
                                             (vf_scr, vp_ref, vc_ref, vn_ref)):
                full[slot, 0:BAND] = before[0, r]
                full[slot, BAND:BAND + n] = cur[0, r]
                full[slot, BAND + n:2 * BAND + n] = after[0, r]
                if pad:
                    full[slot, 2 * BAND + n:2 * BAND + n + pad] = jnp.zeros((pad, GROUP_WIDTH), full.dtype)

        def edge_bias(sub, nsub=nsub, bias_scr=bias_scr):
            first = jnp.logical_and(i == 0, sub == 0)
            final = jnp.logical_and(i == last, sub == nsub - 1)
            return bias_scr[first.astype(jnp.int32) + 2 * final.astype(jnp.int32)]

        def scores(it, j, tq=tq, q_ref=q_ref, block_coords=block_coords, stage_kv=stage_kv,
                   one_class=one_class, edge_bias=edge_bias, transposed=transposed,
                   s_scr=s_scr, m_scr=m_scr):
            r, sub, kv_slot = block_coords(it, j)
            if not one_class and sub == 0:
                stage_kv(r, kv_slot)
            q0 = sub * tq if isinstance(sub, int) else pl.multiple_of(sub * tq, tq)
            bias = edge_bias(sub)
            for p in range(n_pairs):
                cols = slice(p * LANES, (p + 1) * LANES)
                qp = q_ref[0, r, pl.ds(q0, tq), cols]
                kp = kf_scr[kv_slot, pl.ds(q0, tk), cols]
                zeros = jnp.zeros_like(qp)
                q_heads = (jnp.where(head_a, qp, zeros), jnp.where(head_a, zeros, qp))
                if transposed:
                    s = lax.dot_general(kp, jnp.concatenate(q_heads, axis=0),
                                        (((1,), (1,)), ((), ())), preferred_element_type=F32)
                    s = jnp.concatenate([s[:, 0:tq] + bias, s[:, tq:2 * tq] + bias], axis=1)
                    s_scr[j % 2, p] = s
                    m_scr[j % 2, p] = jnp.broadcast_to(jnp.max(s, axis=0, keepdims=True),
                                                       (SUBLANES, 2 * tq))
                else:
                    s = lax.dot_general(jnp.concatenate(q_heads, axis=0), kp, (((1,), (1,)), ((), ())),
                                        preferred_element_type=F32)
                    s = s + jnp.concatenate([bias, bias], axis=0)
                    s_scr[j % 2, p] = s
                    m_scr[j % 2, p] = jnp.broadcast_to(jnp.max(s, axis=-1, keepdims=True),
                                                       (2 * tq, LANES))

        def finish(it, j, tq=tq, dil=dil, g=g, block_coords=block_coords, transposed=transposed,
                   s_scr=s_scr, m_scr=m_scr):
            r, sub, kv_slot = block_coords(it, j)
            q0 = sub * tq if isinstance(sub, int) else pl.multiple_of(sub * tq, tq)
            for p in range(n_pairs):
                cols = slice(p * LANES, (p + 1) * LANES)
                vp = vf_scr[kv_slot, pl.ds(q0, tk), cols]
                if transposed:
                    vt = vp.T
                    ones = jnp.ones((2 * SUBLANES, tk), BF16)
                    m = m_scr[j % 2, p]
                    e = jnp.exp2(s_scr[j % 2, p] - m[0:1]).astype(BF16)
                    tops, lses = [], []
                    for hh in range(2):
                        lhs = jnp.concatenate([vt[hh * HEAD_DIM:(hh + 1) * HEAD_DIM], ones], axis=0)
                        res = jnp.dot(lhs, e[:, hh * tq:(hh + 1) * tq], preferred_element_type=F32)
                        den = res[HEAD_DIM:HEAD_DIM + SUBLANES]
                        tops.append(res[0:HEAD_DIM] * jnp.tile(1.0 / den, (HEAD_DIM // SUBLANES, 1)))
                        lse = m[:, hh * tq:(hh + 1) * tq] + jnp.log2(den)
                        lses.append(jnp.tile(lse, (HEAD_DIM // SUBLANES, 1)))
                    o2 = jnp.concatenate(tops, axis=0).T
                    lse2 = jnp.concatenate(lses, axis=0).T
                else:
                    s = s_scr[j % 2, p]
                    m = m_scr[j % 2, p]
                    e = jnp.concatenate([jnp.exp2(s[:, t * LANES:(t + 1) * LANES] - m)
                                         for t in range(tk // LANES)], axis=1)
                    rhs = jnp.concatenate([vp, jnp.ones_like(vp)], axis=1)
                    res = jnp.dot(e.astype(BF16), rhs, preferred_element_type=F32)
                    den = jnp.where(head_a, res[0:tq, LANES:2 * LANES], res[tq:2 * tq, LANES:2 * LANES])
                    o2 = jnp.where(head_a, res[0:tq, 0:LANES], res[tq:2 * tq, 0:LANES]) / den
                    lse2 = jnp.where(head_a, m[0:tq], m[tq:2 * tq]) + jnp.log2(den)
                if dil == 1:
                    rows = pl.ds(q0, tq)
                elif two_hops:
                    inner = dil // MAX_STRIDE
                    rows = pl.ds((r % MAX_STRIDE) * (tm // MAX_STRIDE) + r // MAX_STRIDE + inner * q0,
                                 tq, stride=inner)
                else:
                    rows = pl.ds(r + dil * q0, tq, stride=dil)
                o_scr[store_g * n_pairs + p, rows, :] = o2
                lse_scr[store_g * n_pairs + p, rows, :] = lse2

        if one_class:
            stage_kv(0, 0)
        scores(0, 0)

        def body(it, carry, per_body=per_body, n_iters=n_iters, scores=scores, finish=finish):
            for j in range(per_body):
                if j + 1 < per_body:
                    scores(it, j + 1)
                else:
                    scores(jnp.minimum(it + 1, n_iters - 1), 0)
                finish(it, j)
            return carry

        lax.fori_loop(0, n_iters, body, 0)

        if two_hops:
            def second_hop(r2, carry, g=g, store_g=store_g):
                per_class = tm // MAX_STRIDE
                for scr in (o_scr, lse_scr):
                    for p in range(n_pairs):
                        for c in range(per_class // LANES):
                            src = pl.ds(pl.multiple_of(r2 * per_class, per_class) + c * LANES, LANES)
                            dst = pl.ds(r2 + MAX_STRIDE * c * LANES, LANES, stride=MAX_STRIDE)
                            scr[g * n_pairs + p, dst, :] = scr[store_g * n_pairs + p, src, :]
                return carry

            lax.fori_loop(0, MAX_STRIDE, second_hop, 0)

    chunk = 128

    def merge_body(c, carry):
        rows = pl.ds(pl.multiple_of(c * chunk, chunk), chunk)
        for p in range(n_pairs):
            ls = [lse_scr[g * n_pairs + p, rows, :] for g in range(N_GROUPS)]
            mx = functools.reduce(jnp.maximum, ls)
            ws = [jnp.exp2(l - mx) for l in ls]
            num = sum(w * o_scr[g * n_pairs + p, rows, :] for g, w in enumerate(ws))
            att_ref[0, rows, p * LANES:(p + 1) * LANES] = (num / sum(ws)).astype(att_ref.dtype)
        return carry

    lax.fori_loop(0, tm // chunk, merge_body, 0)


def _attn(qkv):
    nb = qkv[0].shape[0]
    seq = qkv[0].shape[1] * qkv[0].shape[2]
    tm = ATT_TILE
    operands = []
    in_specs = []
    kv_rows = 0
    for g, dil in enumerate(DILS):
        q, k, v = qkv[3 * g:3 * g + 3]
        n = tm // dil
        nblk = n // BAND
        n_halo = seq // dil // BAND
        cur = pl.BlockSpec((1, dil, n, GROUP_WIDTH), lambda b, i: (b, 0, i, 0))
        prev = pl.BlockSpec((1, dil, BAND, GROUP_WIDTH),
                            lambda b, i, nblk=nblk: (b, 0, jnp.maximum(i * nblk - 1, 0), 0))
        nxt = pl.BlockSpec((1, dil, BAND, GROUP_WIDTH),
                           lambda b, i, nblk=nblk, n_halo=n_halo: (b, 0, jnp.minimum((i + 1) * nblk, n_halo - 1), 0))
        operands += [q, k, k, k, v, v, v]
        in_specs += [cur, prev, cur, nxt, prev, cur, nxt]
        kv_rows = max(kv_rows, n - min(ATT_Q_BLOCK, n) + ATT_K_BLOCK)
    merge_shape = (N_GROUPS * GROUP_WIDTH // LANES, tm, LANES)
    tq_small = min(ATT_Q_BLOCK, tm // max(DILS))
    scratch = [
        pltpu.VMEM(merge_shape, F32),
        pltpu.VMEM(merge_shape, F32),
        pltpu.VMEM((2, kv_rows, GROUP_WIDTH), BF16),
        pltpu.VMEM((2, kv_rows, GROUP_WIDTH), BF16),
        pltpu.VMEM((2, GROUP_WIDTH // LANES, ATT_K_BLOCK, 2 * ATT_Q_BLOCK), F32),
        pltpu.VMEM((2, GROUP_WIDTH // LANES, SUBLANES, 2 * ATT_Q_BLOCK), F32),
        pltpu.VMEM((4, ATT_K_BLOCK, ATT_Q_BLOCK), F32),
        pltpu.VMEM((2, GROUP_WIDTH // LANES, 2 * tq_small, ATT_K_BLOCK), F32),
        pltpu.VMEM((2, GROUP_WIDTH // LANES, 2 * tq_small, LANES), F32),
        pltpu.VMEM((4, tq_small, ATT_K_BLOCK), F32),
    ]
    return pl.pallas_call(
        _attn_kernel,
        grid=(nb, seq // tm),
        in_specs=in_specs,
        out_specs=pl.BlockSpec((1, tm, GROUP_WIDTH), lambda b, i: (b, i, 0)),
        out_shape=jax.ShapeDtypeStruct((nb, seq, GROUP_WIDTH), BF16),
        scratch_shapes=scratch,
        compiler_params=_params(2, "arbitrary"),
        name="attn",
    )(*operands)


def _mix_kernel(x_ref, att_ref, u_ref, up_ref, un_ref, mod_ref, g1_ref, cw_ref, cb_ref,
                w3_ref, wa_ref, wc_ref, wm_ref, o_ref):
    tm = x_ref.shape[1]
    i = pl.program_id(1)
    last = pl.num_programs(1) - 1
    mod = mod_ref[0]
    halo = up_ref.shape[1]
    cw = cw_ref[...]
    rp = PART_ROWS
    n_parts = tm // rp
    row_id = lax.broadcasted_iota(jnp.int32, (rp, 1), 0)

    for part in range(n_parts):
        rows = slice(part * rp, (part + 1) * rp)
        x = x_ref[0, rows]
        xb = _modulated_norm(x, g1_ref[...], mod[1:2], mod[0:1]).astype(BF16)

        u = u_ref[0, rows]
        if part == 0:
            u_before = jnp.where(i > 0, up_ref[0, halo - 1:halo, :], 0.0)
        else:
            u_before = u_ref[0, part * rp - 1:part * rp, :]
        if part == n_parts - 1:
            u_after = jnp.where(i < last, un_ref[0, 0:1, :], 0.0)
        else:
            u_after = u_ref[0, (part + 1) * rp:(part + 1) * rp + 1, :]
        u_m1 = jnp.where(row_id == 0, u_before, pltpu.roll(u, 1, 0))
        u_p1 = jnp.where(row_id == rp - 1, u_after, pltpu.roll(u, rp - 1, 0))
        conv = cb_ref[...] + u_m1 * cw[0:1] + u * cw[1:2] + u_p1 * cw[2:3]

        b_gate = jnp.dot(xb, w3_ref[:, 0:D_MODEL], preferred_element_type=F32)
        cv = jnp.dot((b_gate * conv).astype(BF16), wc_ref[...], preferred_element_type=F32)
        g_att = jnp.dot(xb, w3_ref[:, D_MODEL:2 * D_MODEL], preferred_element_type=F32)
        ap = jnp.dot(att_ref[0, rows], wa_ref[...], preferred_element_type=F32)
        merged = jax.nn.sigmoid(g_att) * ap
        g_conv = jnp.dot(xb, w3_ref[:, 2 * D_MODEL:3 * D_MODEL], preferred_element_type=F32)
        merged = merged + jax.nn.sigmoid(g_conv) * cv
        mix = jnp.dot(merged.astype(BF16), wm_ref[...], preferred_element_type=F32)
        o_ref[0, rows] = x + mod[2:3] * mix


def _mix(x, att, u, mod, g1, conv_w, conv_b, w_bgg, w_att, w_conv, w_mixo):
    nb, seq, _ = x.shape
    tm = QKV_TILE
    halo = 8
    nh = tm // halo
    return pl.pallas_call(
        _mix_kernel,
        grid=(nb, seq // tm),
        in_specs=[
            pl.BlockSpec((1, tm, D_MODEL), lambda b, i: (b, i, 0)),
            pl.BlockSpec((1, tm, GROUP_WIDTH), lambda b, i: (b, i, 0)),
            pl.BlockSpec((1, tm, D_MODEL), lambda b, i: (b, i, 0)),
            pl.BlockSpec((1, halo, D_MODEL), lambda b, i: (b, jnp.maximum(i * nh - 1, 0), 0)),
            pl.BlockSpec((1, halo, D_MODEL), lambda b, i: (b, jnp.minimum((i + 1) * nh, seq // halo - 1), 0)),
            pl.BlockSpec((1, 6, D_MODEL), lambda b, i: (b, 0, 0)),
            pl.BlockSpec((1, D_MODEL), lambda b, i: (0, 0)),
            pl.BlockSpec((CONV_WIDTH, D_MODEL), lambda b, i: (0, 0)),
            pl.BlockSpec((1, D_MODEL), lambda b, i: (0, 0)),
            _resident(w_bgg.shape),
            _resident(w_att.shape),
            _resident(w_conv.shape),
            _resident(w_mixo.shape),
        ],
        out_specs=pl.BlockSpec((1, tm, D_MODEL), lambda b, i: (b, i, 0)),
        out_shape=jax.ShapeDtypeStruct((nb, seq, D_MODEL), F32),
        compiler_params=_params(2),
        name="mix",
    )(x, att, u, u, u, mod, g1, conv_w, conv_b, w_bgg, w_att, w_conv, w_mixo)


def _mlp_kernel(x_ref, mod_ref, g2_ref, gf_ref, w1_ref, w2_ref, o_ref):
    mod = mod_ref[0]
    tm = x_ref.shape[1]
    rp = PART_ROWS
    ff_chunk = D_MODEL
    for part in range(tm // rp):
        rows = slice(part * rp, (part + 1) * rp)
        x = x_ref[0, rows]
        xb = _modulated_norm(x, g2_ref[...], mod[4:5], mod[3:4]).astype(BF16)
        acc = jnp.zeros(x.shape, F32)
        for c in range(D_FF // ff_chunk):
            cols = slice(c * ff_chunk, (c + 1) * ff_chunk)
            h = jnp.dot(xb, w1_ref[:, cols], preferred_element_type=F32)
            h = jnp.square(jnp.maximum(h, 0.0)).astype(BF16)
            acc = acc + jnp.dot(h, w2_ref[cols, :], preferred_element_type=F32)
        x2 = x + mod[5:6] * acc
        r = lax.rsqrt(jnp.mean(x2 * x2, axis=-1, keepdims=True) + EPS)
        o_ref[0, rows] = (x2 * r) * gf_ref[...]


def _mlp(x1, mod, g2, gf, w1, w2):
    nb, seq, _ = x1.shape
    tm = MLP_TILE
    return pl.pallas_call(
        _mlp_kernel,
        grid=(nb, seq // tm),
        in_specs=[
            pl.BlockSpec((1, tm, D_MODEL), lambda b, i: (b, i, 0)),
            pl.BlockSpec((1, 6, D_MODEL), lambda b, i: (b, 0, 0)),
            pl.BlockSpec((1, D_MODEL), lambda b, i: (0, 0)),
            pl.BlockSpec((1, D_MODEL), lambda b, i: (0, 0)),
            _resident(w1.shape),
            _resident(w2.shape),
        ],
        out_specs=pl.BlockSpec((1, tm, D_MODEL), lambda b, i: (b, i, 0)),
        out_shape=jax.ShapeDtypeStruct((nb, seq, D_MODEL), F32),
        compiler_params=_params(2),
        name="mlp",
    )(x1, mod, g2, gf, w1, w2)


def _rope_tables(seq, tm):
    inv = 1.0 / (ROPE_THETA ** (np.arange(HALF_ROT, dtype=np.float64) / HALF_ROT))
    inv = np.tile(inv, LANES // HALF_ROT)[None, :]
    sign = np.where((np.arange(LANES) // HALF_ROT) % 2 == 0, -1.0, 1.0)[None, :]
    offsets = []
    for dil in DILS:
        o = np.arange(tm).reshape(tm // PART_ROWS, PART_ROWS // dil, dil)
        offsets.append(o.transpose(0, 2, 1).reshape(tm))
    ang_b = np.stack(offsets).astype(np.float64)[:, :, None] * inv[None]
    ang_a = (np.arange(seq // tm, dtype=np.float64) * tm)[:, None] * inv
    base = np.stack([np.cos(ang_b), np.sin(ang_b), sign * np.sin(ang_b)], axis=1)
    tile = np.stack([np.cos(ang_a), np.sin(ang_a), sign * np.sin(ang_a)], axis=1)
    return jnp.asarray(base, F32), jnp.asarray(tile, F32)


def _encoder(x, mod, g1, g2, gf, conv_w, conv_b, w_in, w_bgg, w_att, w_conv, w_mixo, w1, w2):
    outs = _qkv(x, mod, g1, w_in)
    att = _attn(outs[:3 * N_GROUPS])
    u = outs[3 * N_GROUPS]
    x1 = _mix(x, att, u, mod, g1, conv_w, conv_b, w_bgg, w_att, w_conv, w_mixo)
    return _mlp(x1, mod, g2, gf, w1, w2)


def kernel(x_prompt, x_sample, c_prompt, c_sample, w_ada, b_ada, norm1_g, w_in, conv_w, conv_b,
           w_attn_out, w_conv_out, w_mix_out, norm2_g, w_mlp_in, w_mlp_out, final_norm_g):
    assert w_ada.shape[0] == 1, "single layer"
    nb_p, nb_s = x_prompt.shape[0], x_sample.shape[0]
    pad = -(nb_p + nb_s) % 8
    c_all = jnp.concatenate([c_prompt, c_sample, jnp.zeros((pad, D_MODEL), F32)], axis=0)
    mod = _ada(c_all, w_ada[0], b_ada[0][None, :])
    mod = mod.reshape(mod.shape[0], 6, D_MODEL)

    w_in_b = w_in[0].astype(BF16)
    w_bgg = w_in_b[:, 3 * ATT_WIDTH + 2 * D_MODEL:]
    shared = (norm1_g, norm2_g, final_norm_g[None, :], conv_w[0], conv_b, w_in_b, w_bgg,
              w_attn_out[0].astype(BF16), w_conv_out[0].astype(BF16), w_mix_out[0].astype(BF16),
              w_mlp_in[0].astype(BF16), w_mlp_out[0].astype(BF16))

    ys = []
    for x, m in ((x_prompt, mod[:nb_p]), (x_sample, mod[nb_p:nb_p + nb_s])):
        ys.append(_encoder(x, m, *shared))
    return tuple(ys)
```

```python
import functools

import jax
import jax.numpy as jnp
import numpy as np
from jax import lax
from jax.experimental import pallas as pl
from jax.experimental.pallas import tpu as pltpu

D_MODEL = 1024
HEAD_DIM = 64
HEADS_PER_GROUP = 8
GROUP_WIDTH = HEADS_PER_GROUP * HEAD_DIM
DILATION_GROUPS = ((128, 1), (512, 4), (2048, 16))
DILS = tuple(d for _, d in DILATION_GROUPS)
BAND = DILATION_GROUPS[0][0] // (2 * DILATION_GROUPS[0][1])
assert all(w // (2 * d) == BAND for w, d in DILATION_GROUPS)
N_GROUPS = len(DILS)
ATT_WIDTH = N_GROUPS * GROUP_WIDTH
CONV_WIDTH = 3
D_FF = 4 * D_MODEL
ROPE_THETA = 10000.0
EPS = 1e-6
NEG = -1e30

LANES = 128
SUBLANES = 8
HALF_ROT = HEAD_DIM // 2
QK_SCALE = HEAD_DIM ** -0.5 * np.log2(np.e)

QKV_TILE = 1024
MLP_TILE = 1024
PART_ROWS = 256
ATT_TILE = 1024
ATT_Q_BLOCK = 2 * BAND
ATT_K_BLOCK = 4 * BAND
MAX_STRIDE = 4
ATT_BLOCKS_PER_BODY = (4, 4, 8)
VMEM_LIMIT = 60 * 1024 * 1024

F32 = jnp.float32
BF16 = jnp.bfloat16


def _resident(shape):
    return pl.BlockSpec(shape, lambda *_: (0,) * len(shape), pipeline_mode=pl.Buffered(1))


def _params(n_axes, semantics="parallel"):
    return pltpu.CompilerParams(
        dimension_semantics=(semantics,) * n_axes, vmem_limit_bytes=VMEM_LIMIT)


def _modulated_norm(x, gain, scale, shift):
    r = lax.rsqrt(jnp.mean(x * x, axis=-1, keepdims=True) + EPS)
    return (x * r) * (gain * (1.0 + scale)) + shift


def _ada_kernel(c_ref, w_ref, b_ref, o_ref):
    c = c_ref[...]
    s = c * jax.nn.sigmoid(c)
    o_ref[...] = jnp.dot(s, w_ref[...], precision=lax.Precision.HIGHEST,
                         preferred_element_type=F32) + b_ref[...]


def _ada(c_all, w_ada, b_ada):
    rows, _ = c_all.shape
    n_out = w_ada.shape[1]
    bn = n_out // 4
    return pl.pallas_call(
        _ada_kernel,
        grid=(n_out // bn,),
        in_specs=[
            pl.BlockSpec((rows, D_MODEL), lambda j: (0, 0)),
            pl.BlockSpec((D_MODEL, bn), lambda j: (0, j)),
            pl.BlockSpec((1, bn), lambda j: (0, j)),
        ],
        out_specs=pl.BlockSpec((rows, bn), lambda j: (0, j)),
        out_shape=jax.ShapeDtypeStruct((rows, n_out), F32),
        compiler_params=_params(1),
        name="ada",
    )(c_all, w_ada, b_ada)


def _qkv_kernel(x_ref, mod_ref, g1_ref, rope_base_ref, rope_tile_ref, w_ref, *rest):
    qkv_refs = rest[:3 * N_GROUPS]
    u_ref = rest[3 * N_GROUPS]
    scr_ref = rest[3 * N_GROUPS + 1]
    tm = x_ref.shape[1]
    mod = mod_ref[0]
    ta = rope_tile_ref[0]
    ca, sa, ssa = ta[0:1], ta[1:2], ta[2:3]
    lane = lax.broadcasted_iota(jnp.int32, (1, LANES), 1)
    first_half = (lane // HALF_ROT) % 2 == 0

    rp = PART_ROWS
    for part in range(tm // rp):
        rows = slice(part * rp, (part + 1) * rp)
        xn = _modulated_norm(x_ref[0, rows], g1_ref[...], mod[1:2], mod[0:1])
        xb = xn.astype(BF16)

        lhs = {1: xb}
        prev_dil, prev = 1, xn
        for k, dil in enumerate(DILS[1:]):
            step = dil // prev_dil
            assert dil % prev_dil == 0 and step <= MAX_STRIDE
            n = rp // dil
            for c in range(D_MODEL // LANES):
                scr_ref[part % 2, k, c] = prev[:, c * LANES:(c + 1) * LANES]
            prev = jnp.concatenate(
                [jnp.concatenate([scr_ref[part % 2, k, c,
                                          pl.ds((r % prev_dil) * (rp // prev_dil) + r // prev_dil, n, stride=step), :]
                                  for r in range(dil)], axis=0)
                 for c in range(D_MODEL // LANES)], axis=1)
            lhs[dil] = prev.astype(BF16)
            prev_dil = dil

        def emit(out_ref, dil, val, part=part, rp=rp):
            n = rp // dil
            for r in range(dil):
                out_ref[0, r, part * n:(part + 1) * n] = val[r * n:(r + 1) * n].astype(BF16)

        def rope(val, scale, g, rows=rows):
            cb, sb, ssb = rope_base_ref[g, 0, rows], rope_base_ref[g, 1, rows], rope_base_ref[g, 2, rows]
            cos = (ca * cb - sa * sb) * scale
            sin = (ssa * cb + ca * ssb) * scale
            parts = []
            for p in range(GROUP_WIDTH // LANES):
                xp = val[:, p * LANES:(p + 1) * LANES]
                partner = jnp.where(first_half, pltpu.roll(xp, LANES - HALF_ROT, 1),
                                    pltpu.roll(xp, HALF_ROT, 1))
                parts.append(xp * cos + partner * sin)
            return jnp.concatenate(parts, axis=1)

        for which in range(3):
            for g, dil in enumerate(DILS):
                col = which * ATT_WIDTH + g * GROUP_WIDTH
                val = jnp.dot(lhs[dil], w_ref[:, col:col + GROUP_WIDTH], preferred_element_type=F32)
                if which < 2:
                    val = rope(val, QK_SCALE if which == 0 else 1.0, g)
                emit(qkv_refs[3 * g + which], dil, val)

        col = 3 * ATT_WIDTH
        h = jnp.dot(xb, w_ref[:, col:col + D_MODEL], preferred_element_type=F32)
        cg = jnp.dot(xb, w_ref[:, col + D_MODEL:col + 2 * D_MODEL], preferred_element_type=F32)
        u_ref[0, rows] = cg * h


def _qkv(x, mod, g1, w_in):
    nb, seq, _ = x.shape
    tm = QKV_TILE
    n_cols = 3 * ATT_WIDTH + 2 * D_MODEL
    rope_base, rope_tile = _rope_tables(seq, tm)
    out_shapes = []
    out_specs = []
    for dil in DILS:
        for _ in range(3):
            out_shapes.append(jax.ShapeDtypeStruct((nb, dil, seq // dil, GROUP_WIDTH), BF16))
            out_specs.append(pl.BlockSpec((1, dil, tm // dil, GROUP_WIDTH), lambda b, i: (b, 0, i, 0)))
    out_shapes.append(jax.ShapeDtypeStruct((nb, seq, D_MODEL), F32))
    out_specs.append(pl.BlockSpec((1, tm, D_MODEL), lambda b, i: (b, i, 0)))
    return pl.pallas_call(
        _qkv_kernel,
        grid=(nb, seq // tm),
        in_specs=[
            pl.BlockSpec((1, tm, D_MODEL), lambda b, i: (b, i, 0)),
            pl.BlockSpec((1, 6, D_MODEL), lambda b, i: (b, 0, 0)),
            pl.BlockSpec((1, D_MODEL), lambda b, i: (0, 0)),
            _resident((N_GROUPS, 3, tm, LANES)),
            pl.BlockSpec((1, 3, LANES), lambda b, i: (i, 0, 0)),
            pl.BlockSpec((D_MODEL, n_cols), lambda b, i: (0, 0), pipeline_mode=pl.Buffered(1)),
        ],
        out_specs=out_specs,
        out_shape=out_shapes,
        scratch_shapes=[pltpu.VMEM((2, N_GROUPS - 1, D_MODEL // LANES, PART_ROWS, LANES), F32)],
        compiler_params=_params(2),
        name="qkv",
    )(x, mod, g1, rope_base, rope_tile, w_in)


def _attn_kernel(*refs):
    in_refs = refs[:7 * N_GROUPS]
    att_ref = refs[7 * N_GROUPS]
    (o_scr, lse_scr, kf_scr, vf_scr, st_scr, mt_scr, biast_scr,
     sr_scr, mr_scr, biasr_scr) = refs[7 * N_GROUPS + 1:]
    tm = att_ref.shape[1]
    i = pl.program_id(1)
    last = pl.num_programs(1) - 1

    n_pairs = GROUP_WIDTH // LANES
    lane = lax.broadcasted_iota(jnp.int32, (1, LANES), 1)
    head_a = lane < HEAD_DIM
    tk = ATT_K_BLOCK
    mask_forms = {}

    def build_masks():
        for (transposed, tq), bias_scr in mask_forms.items():
            if transposed:
                key = lax.broadcasted_iota(jnp.int32, (tk, tq), 0)
                qry = lax.broadcasted_iota(jnp.int32, (tk, tq), 1)
            else:
                qry = lax.broadcasted_iota(jnp.int32, (tq, tk), 0)
                key = lax.broadcasted_iota(jnp.int32, (tq, tk), 1)
            rel = key - BAND - qry
            band = (rel >= -BAND) & (rel <= BAND)
            after_start = key >= BAND
            before_end = key < tq + BAND
            for variant, valid in enumerate((band, band & after_start, band & before_end,
                                             band & after_start & before_end)):
                bias_scr[variant] = jnp.where(valid, 0.0, NEG).astype(F32)

    for g, dil in enumerate(DILS):
        tq = min(ATT_Q_BLOCK, tm // dil)
        mask_forms.setdefault((tq == LANES, tq), biast_scr if tq == LANES else biasr_scr)
    pl.when(jnp.logical_and(pl.program_id(0) == 0, i == 0))(build_masks)

    for g in sorted(range(N_GROUPS), key=lambda g: -DILS[g]):
        dil = DILS[g]
        q_ref, kp_ref, kc_ref, kn_ref, vp_ref, vc_ref, vn_ref = in_refs[7 * g:7 * g + 7]
        two_hops = dil > MAX_STRIDE
        if two_hops:
            assert dil % MAX_STRIDE == 0 and dil // MAX_STRIDE <= MAX_STRIDE and MAX_STRIDE in DILS
        store_g = DILS.index(MAX_STRIDE) if two_hops else g
        n = tm // dil
        tq = min(ATT_Q_BLOCK, n)
        nsub = n // tq
        per_body = ATT_BLOCKS_PER_BODY[g]
        n_iters = dil * nsub // per_body
        one_class = nsub >= per_body
        if one_class:
            assert dil == 1 and nsub % per_body == 0
        else:
            assert per_body % (2 * nsub) == 0 and dil % (per_body // nsub) == 0

        transposed = tq == LANES
        s_scr, m_scr, bias_scr = (st_scr, mt_scr, biast_scr) if transposed else (sr_scr, mr_scr, biasr_scr)

        def block_coords(it, j, nsub=nsub, per_body=per_body, one_class=one_class):
            if one_class:
                return 0, it * per_body + j, 0
            return it * (per_body // nsub) + j // nsub, j % nsub, (j // nsub) % 2

        def stage_kv(r, slot, n=n, tq=tq, kp_ref=kp_ref, kc_ref=kc_ref, kn_ref=kn_ref,
                     vp_ref=vp_ref, vc_ref=vc_ref, vn_ref=vn_ref):
            pad = tk - tq - 2 * BAND
            for full, before, cur, after in ((kf_scr, kp_ref, kc_ref, kn_ref),
                                             (vf_scr, vp_ref, vc_ref, vn_ref)):
                full[slot, 0:BAND] = before[0, r]
                full[slot, BAND:BAND + n] = cur[0, r]
                full[slot, BAND + n:2 * BAND + n] = after[0, r]
                if pad:
                    full[slot, 2 * BAND + n:2 * BAND + n + pad] = jnp.zeros((pad, GROUP_WIDTH), full.dtype)

        def edge_bias(sub, nsub=nsub, bias_scr=bias_scr):
            first = jnp.logical_and(i == 0, sub == 0)
            final = jnp.logical_and(i == last, sub == nsub - 1)
            return bias_scr[first.astype(jnp.int32) + 2 * final.astype(jnp.int32)]

        def scores(it, j, tq=tq, q_ref=q_ref, block_coords=block_coords, stage_kv=stage_kv,
                   one_class=one_class, edge_bias=edge_bias, transposed=transposed,
                   s_scr=s_scr, m_scr=m_scr):
            r, sub, kv_slot = block_coords(it, j)
            if not one_class and sub == 0:
                stage_kv(r, kv_slot)
            q0 = sub * tq if isinstance(sub, int) else pl.multiple_of(sub * tq, tq)
            bias = edge_bias(sub)
            for p in range(n_pairs):
                cols = slice(p * LANES, (p + 1) * LANES)
                qp = q_ref[0, r, pl.ds(q0, tq), cols]
                kp = kf_scr[kv_slot, pl.ds(q0, tk), cols]
                zeros = jnp.zeros_like(qp)
                q_heads = (jnp.where(head_a, qp, zeros), jnp.where(head_a, zeros, qp))
                if transposed:
                    s = lax.dot_general(kp, jnp.concatenate(q_heads, axis=0),
                                        (((1,), (1,)), ((), ())), preferred_element_type=F32)
                    s = jnp.concatenate([s[:, 0:tq] + bias, s[:, tq:2 * tq] + bias], axis=1)
                    s_scr[j % 2, p] = s
                    m_scr[j % 2, p] = jnp.broadcast_to(jnp.max(s, axis=0, keepdims=True),
                                                       (SUBLANES, 2 * tq))
                else:
                    s = lax.dot_general(jnp.concatenate(q_heads, axis=0), kp, (((1,), (1,)), ((), ())),
                                        preferred_element_type=F32)
                    s = s + jnp.concatenate([bias, bias], axis=0)
                    s_scr[j % 2, p] = s
                    m_scr[j % 2, p] = jnp.broadcast_to(jnp.max(s, axis=-1, keepdims=True),
                                                       (2 * tq, LANES))

        def finish(it, j, tq=tq, dil=dil, g=g, block_coords=block_coords, transposed=transposed,
                   s_scr=s_scr, m_scr=m_scr):
            r, sub, kv_slot = block_coords(it, j)
            q0 = sub * tq if isinstance(sub, int) else pl.multiple_of(sub * tq, tq)
            for p in range(n_pairs):
                cols = slice(p * LANES, (p + 1) * LANES)
                vp = vf_scr[kv_slot, pl.ds(q0, tk), cols]
                if transposed:
                    vt = vp.T
                    ones = jnp.ones((2 * SUBLANES, tk), BF16)
                    m = m_scr[j % 2, p]
                    e = jnp.exp2(s_scr[j % 2, p] - m[0:1]).astype(BF16)
                    tops, lses = [], []
                    for hh in range(2):
                        lhs = jnp.concatenate([vt[hh * HEAD_DIM:(hh + 1) * HEAD_DIM], ones], axis=0)
                        res = jnp.dot(lhs, e[:, hh * tq:(hh + 1) * tq], preferred_element_type=F32)
                        den = res[HEAD_DIM:HEAD_DIM + SUBLANES]
                        tops.append(res[0:HEAD_DIM] * jnp.tile(1.0 / den, (HEAD_DIM // SUBLANES, 1)))
                        lse = m[:, hh * tq:(hh + 1) * tq] + jnp.log2(den)
                        lses.append(jnp.tile(lse, (HEAD_DIM // SUBLANES, 1)))
                    o2 = jnp.concatenate(tops, axis=0).T
                    lse2 = jnp.concatenate(lses, axis=0).T
                else:
                    s = s_scr[j % 2, p]
                    m = m_scr[j % 2, p]
                    e = jnp.concatenate([jnp.exp2(s[:, t * LANES:(t + 1) * LANES] - m)
                                         for t in range(tk // LANES)], axis=1)
                    rhs = jnp.concatenate([vp, jnp.ones_like(vp)], axis=1)
                    res = jnp.dot(e.astype(BF16), rhs, preferred_element_type=F32)
                    den = jnp.where(head_a, res[0:tq, LANES:2 * LANES], res[tq:2 * tq, LANES:2 * LANES])
                    o2 = jnp.where(head_a, res[0:tq, 0:LANES], res[tq:2 * tq, 0:LANES]) / den
                    lse2 = jnp.where(head_a, m[0:tq], m[tq:2 * tq]) + jnp.log2(den)
                if dil == 1:
                    rows = pl.ds(q0, tq)
                elif two_hops:
                    inner = dil // MAX_STRIDE
                    rows = pl.ds((r % MAX_STRIDE) * (tm // MAX_STRIDE) + r // MAX_STRIDE + inner * q0,
                                 tq, stride=inner)
                else:
                    rows = pl.ds(r + dil * q0, tq, stride=dil)
                o_scr[store_g * n_pairs + p, rows, :] = o2
                lse_scr[store_g * n_pairs + p, rows, :] = lse2

        if one_class:
            stage_kv(0, 0)
        scores(0, 0)

        def body(it, carry, per_body=per_body, n_iters=n_iters, scores=scores, finish=finish):
            for j in range(per_body):
                if j + 1 < per_body:
                    scores(it, j + 1)
                else:
                    scores(jnp.minimum(it + 1, n_iters - 1), 0)
                finish(it, j)
            return carry

        lax.fori_loop(0, n_iters, body, 0)

        if two_hops:
            def second_hop(r2, carry, g=g, store_g=store_g):
                per_class = tm // MAX_STRIDE
                for scr in (o_scr, lse_scr):
                    for p in range(n_pairs):
                        for c in range(per_class // LANES):
                            src = pl.ds(pl.multiple_of(r2 * per_class, per_class) + c * LANES, LANES)
                            dst = pl.ds(r2 + MAX_STRIDE * c * LANES, LANES, stride=MAX_STRIDE)
                            scr[g * n_pairs + p, dst, :] = scr[store_g * n_pairs + p, src, :]
                return carry

            lax.fori_loop(0, MAX_STRIDE, second_hop, 0)

    chunk = 128

    def merge_body(c, carry):
        rows = pl.ds(pl.multiple_of(c * chunk, chunk), chunk)
        for p in range(n_pairs):
            ls = [lse_scr[g * n_pairs + p, rows, :] for g in range(N_GROUPS)]
            mx = functools.reduce(jnp.maximum, ls)
            ws = [jnp.exp2(l - mx) for l in ls]
            num = sum(w * o_scr[g * n_pairs + p, rows, :] for g, w in enumerate(ws))
            att_ref[0, rows, p * LANES:(p + 1) * LANES] = (num / sum(ws)).astype(att_ref.dtype)
        return carry

    lax.fori_loop(0, tm // chunk, merge_body, 0)


def _attn(qkv):
    nb = qkv[0].shape[0]
    seq = qkv[0].shape[1] * qkv[0].shape[2]
    tm = ATT_TILE
    operands = []
    in_specs = []
    kv_rows = 0
    for g, dil in enumerate(DILS):
        q, k, v = qkv[3 * g:3 * g + 3]
        n = tm // dil
        nblk = n // BAND
        n_halo = seq // dil // BAND
        cur = pl.BlockSpec((1, dil, n, GROUP_WIDTH), lambda b, i: (b, 0, i, 0))
        prev = pl.BlockSpec((1, dil, BAND, GROUP_WIDTH),
                            lambda b, i, nblk=nblk: (b, 0, jnp.maximum(i * nblk - 1, 0), 0))
        nxt = pl.BlockSpec((1, dil, BAND, GROUP_WIDTH),
                           lambda b, i, nblk=nblk, n_halo=n_halo: (b, 0, jnp.minimum((i + 1) * nblk, n_halo - 1), 0))
        operands += [q, k, k, k, v, v, v]
        in_specs += [cur, prev, cur, nxt, prev, cur, nxt]
        kv_rows = max(kv_rows, n - min(ATT_Q_BLOCK, n) + ATT_K_BLOCK)
    merge_shape = (N_GROUPS * GROUP_WIDTH // LANES, tm, LANES)
    tq_small = min(ATT_Q_BLOCK, tm // max(DILS))
    scratch = [
        pltpu.VMEM(merge_shape, F32),
        pltpu.VMEM(merge_shape, F32),
        pltpu.VMEM((2, kv_rows, GROUP_WIDTH), BF16),
        pltpu.VMEM((2, kv_rows, GROUP_WIDTH), BF16),
        pltpu.VMEM((2, GROUP_WIDTH // LANES, ATT_K_BLOCK, 2 * ATT_Q_BLOCK), F32),
        pltpu.VMEM((2, GROUP_WIDTH // LANES, SUBLANES, 2 * ATT_Q_BLOCK), F32),
        pltpu.VMEM((4, ATT_K_BLOCK, ATT_Q_BLOCK), F32),
        pltpu.VMEM((2, GROUP_WIDTH // LANES, 2 * tq_small, ATT_K_BLOCK), F32),
        pltpu.VMEM((2, GROUP_WIDTH // LANES, 2 * tq_small, LANES), F32),
        pltpu.VMEM((4, tq_small, ATT_K_BLOCK), F32),
    ]
    return pl.pallas_call(
        _attn_kernel,
        grid=(nb, seq // tm),
        in_specs=in_specs,
        out_specs=pl.BlockSpec((1, tm, GROUP_WIDTH), lambda b, i: (b, i, 0)),
        out_shape=jax.ShapeDtypeStruct((nb, seq, GROUP_WIDTH), BF16),
        scratch_shapes=scratch,
        compiler_params=_params(2, "arbitrary"),
        name="attn",
    )(*operands)


def _mix_kernel(x_ref, att_ref, u_ref, up_ref, un_ref, mod_ref, g1_ref, cw_ref, cb_ref,
                w3_ref, wa_ref, wc_ref, wm_ref, o_ref):
    tm = x_ref.shape[1]
    i = pl.program_id(1)
    last = pl.num_programs(1) - 1
    mod = mod_ref[0]
    halo = up_ref.shape[1]
    cw = cw_ref[...]
    rp = PART_ROWS
    n_parts = tm // rp
    row_id = lax.broadcasted_iota(jnp.int32, (rp, 1), 0)

    for part in range(n_parts):
        rows = slice(part * rp, (part + 1) * rp)
        x = x_ref[0, rows]
        xb = _modulated_norm(x, g1_ref[...], mod[1:2], mod[0:1]).astype(BF16)

        u = u_ref[0, rows]
        if part == 0:
            u_before = jnp.where(i > 0, up_ref[0, halo - 1:halo, :], 0.0)
        else:
            u_before = u_ref[0, part * rp - 1:part * rp, :]
        if part == n_parts - 1:
            u_after = jnp.where(i < last, un_ref[0, 0:1, :], 0.0)
        else:
            u_after = u_ref[0, (part + 1) * rp:(part + 1) * rp + 1, :]
        u_m1 = jnp.where(row_id == 0, u_before, pltpu.roll(u, 1, 0))
        u_p1 = jnp.where(row_id == rp - 1, u_after, pltpu.roll(u, rp - 1, 0))
        conv = cb_ref[...] + u_m1 * cw[0:1] + u * cw[1:2] + u_p1 * cw[2:3]

        b_gate = jnp.dot(xb, w3_ref[:, 0:D_MODEL], preferred_element_type=F32)
        cv = jnp.dot((b_gate * conv).astype(BF16), wc_ref[...], preferred_element_type=F32)
        g_att = jnp.dot(xb, w3_ref[:, D_MODEL:2 * D_MODEL], preferred_element_type=F32)
        ap = jnp.dot(att_ref[0, rows], wa_ref[...], preferred_element_type=F32)
        merged = jax.nn.sigmoid(g_att) * ap
        g_conv = jnp.dot(xb, w3_ref[:, 2 * D_MODEL:3 * D_MODEL], preferred_element_type=F32)
        merged = merged + jax.nn.sigmoid(g_conv) * cv
        mix = jnp.dot(merged.astype(BF16), wm_ref[...], preferred_element_type=F32)
        o_ref[0, rows] = x + mod[2:3] * mix


def _mix(x, att, u, mod, g1, conv_w, conv_b, w_bgg, w_att, w_conv, w_mixo):
    nb, seq, _ = x.shape
    tm = MLP_TILE
    halo = 8
    nh = tm // halo
    return pl.pallas_call(
        _mix_kernel,
        grid=(nb, seq // tm),
        in_specs=[
            pl.BlockSpec((1, tm, D_MODEL), lambda b, i: (b, i, 0)),
            pl.BlockSpec((1, tm, GROUP_WIDTH), lambda b, i: (b, i, 0)),
            pl.BlockSpec((1, tm, D_MODEL), lambda b, i: (b, i, 0)),
            pl.BlockSpec((1, halo, D_MODEL), lambda b, i: (b, jnp.maximum(i * nh - 1, 0), 0)),
            pl.BlockSpec((1, halo, D_MODEL), lambda b, i: (b, jnp.minimum((i + 1) * nh, seq // halo - 1), 0)),
            pl.BlockSpec((1, 6, D_MODEL), lambda b, i: (b, 0, 0)),
            pl.BlockSpec((1, D_MODEL), lambda b, i: (0, 0)),
            pl.BlockSpec((CONV_WIDTH, D_MODEL), lambda b, i: (0, 0)),
            pl.BlockSpec((1, D_MODEL), lambda b, i: (0, 0)),
            _resident(w_bgg.shape),
            _resident(w_att.shape),
            _resident(w_conv.shape),
            _resident(w_mixo.shape),
        ],
        out_specs=pl.BlockSpec((1, tm, D_MODEL), lambda b, i: (b, i, 0)),
        out_shape=jax.ShapeDtypeStruct((nb, seq, D_MODEL), F32),
        compiler_params=_params(2),
        name="mix",
    )(x, att, u, u, u, mod, g1, conv_w, conv_b, w_bgg, w_att, w_conv, w_mixo)


def _mlp_kernel(x_ref, mod_ref, g2_ref, gf_ref, w1_ref, w2_ref, o_ref):
    mod = mod_ref[0]
    tm = x_ref.shape[1]
    rp = PART_ROWS
    ff_chunk = D_MODEL
    for part in range(tm // rp):
        rows = slice(part * rp, (part + 1) * rp)
        x = x_ref[0, rows]
        xb = _modulated_norm(x, g2_ref[...], mod[4:5], mod[3:4]).astype(BF16)
        acc = jnp.zeros(x.shape, F32)
        for c in range(D_FF // ff_chunk):
            cols = slice(c * ff_chunk, (c + 1) * ff_chunk)
            h = jnp.dot(xb, w1_ref[:, cols], preferred_element_type=F32)
            h = jnp.square(jnp.maximum(h, 0.0)).astype(BF16)
            acc = acc + jnp.dot(h, w2_ref[cols, :], preferred_element_type=F32)
        x2 = x + mod[5:6] * acc
        r = lax.rsqrt(jnp.mean(x2 * x2, axis=-1, keepdims=True) + EPS)
        o_ref[0, rows] = (x2 * r) * gf_ref[...]


def _mlp(x1, mod, g2, gf, w1, w2):
    nb, seq, _ = x1.shape
    tm = MLP_TILE
    return pl.pallas_call(
        _mlp_kernel,
        grid=(nb, seq // tm),
        in_specs=[
            pl.BlockSpec((1, tm, D_MODEL), lambda b, i: (b, i, 0)),
            pl.BlockSpec((1, 6, D_MODEL), lambda b, i: (b, 0, 0)),
            pl.BlockSpec((1, D_MODEL), lambda b, i: (0, 0)),
            pl.BlockSpec((1, D_MODEL), lambda b, i: (0, 0)),
            _resident(w1.shape),
            _resident(w2.shape),
        ],
        out_specs=pl.BlockSpec((1, tm, D_MODEL), lambda b, i: (b, i, 0)),
        out_shape=jax.ShapeDtypeStruct((nb, seq, D_MODEL), F32),
        compiler_params=_params(2),
        name="mlp",
    )(x1, mod, g2, gf, w1, w2)


def _rope_tables(seq, tm):
    inv = 1.0 / (ROPE_THETA ** (np.arange(HALF_ROT, dtype=np.float64) / HALF_ROT))
    inv = np.tile(inv, LANES // HALF_ROT)[None, :]
    sign = np.where((np.arange(LANES) // HALF_ROT) % 2 == 0, -1.0, 1.0)[None, :]
    offsets = []
    for dil in DILS:
        o = np.arange(tm).reshape(tm // PART_ROWS, PART_ROWS // dil, dil)
        offsets.append(o.transpose(0, 2, 1).reshape(tm))
    ang_b = np.stack(offsets).astype(np.float64)[:, :, None] * inv[None]
    ang_a = (np.arange(seq // tm, dtype=np.float64) * tm)[:, None] * inv
    base = np.stack([np.cos(ang_b), np.sin(ang_b), sign * np.sin(ang_b)], axis=1)
    tile = np.stack([np.cos(ang_a), np.sin(ang_a), sign * np.sin(ang_a)], axis=1)
    return jnp.asarray(base, F32), jnp.asarray(tile, F32)


def _encoder(x, mod, g1, g2, gf, conv_w, conv_b, w_in, w_bgg, w_att, w_conv, w_mixo, w1, w2):
    outs = _qkv(x, mod, g1, w_in)
    att = _attn(outs[:3 * N_GROUPS])
    u = outs[3 * N_GROUPS]
    x1 = _mix(x, att, u, mod, g1, conv_w, conv_b, w_bgg, w_att, w_conv, w_mixo)
    return _mlp(x1, mod, g2, gf, w1, w2)


def kernel(x_prompt, x_sample, c_prompt, c_sample, w_ada, b_ada, norm1_g, w_in, conv_w, conv_b,
           w_attn_out, w_conv_out, w_mix_out, norm2_g, w_mlp_in, w_mlp_out, final_norm_g):
    assert w_ada.shape[0] == 1, "single layer"
    nb_p, nb_s = x_prompt.shape[0], x_sample.shape[0]
    pad = -(nb_p + nb_s) % 8
    c_all = jnp.concatenate([c_prompt, c_sample, jnp.zeros((pad, D_MODEL), F32)], axis=0)
    mod = _ada(c_all, w_ada[0], b_ada[0][None, :])
    mod = mod.reshape(mod.shape[0], 6, D_MODEL)

    w_in_b = w_in[0].astype(BF16)
    w_bgg = w_in_b[:, 3 * ATT_WIDTH + 2 * D_MODEL:]
    shared = (norm1_g, norm2_g, final_norm_g[None, :], conv_w[0], conv_b, w_in_b, w_bgg,
              w_attn_out[0].astype(BF16), w_conv_out[0].astype(BF16), w_mix_out[0].astype(BF16),
              w_mlp_in[0].astype(BF16), w_mlp_out[0].astype(BF16))

    ys = []
    for x, m in ((x_prompt, mod[:nb_p]), (x_sample, mod[nb_p:nb_p + nb_s])):
        ys.append(_encoder(x, m, *shared))
    return tuple(ys)
```

```python
import functools

import jax
import jax.numpy as jnp
import numpy as np
from jax import lax
from jax.experimental import pallas as pl
from jax.experimental.pallas import tpu as pltpu

D_MODEL = 1024
HEAD_DIM = 64
HEADS_PER_GROUP = 8
GROUP_WIDTH = HEADS_PER_GROUP * HEAD_DIM
DILATION_GROUPS = ((128, 1), (512, 4), (2048, 16))
DILS = tuple(d for _, d in DILATION_GROUPS)
BAND = DILATION_GROUPS[0][0] // (2 * DILATION_GROUPS[0][1])
assert all(w // (2 * d) == BAND for w, d in DILATION_GROUPS)
N_GROUPS = len(DILS)
ATT_WIDTH = N_GROUPS * GROUP_WIDTH
CONV_WIDTH = 3
D_FF = 4 * D_MODEL
ROPE_THETA = 10000.0
EPS = 1e-6
NEG = -1e30

LANES = 128
SUBLANES = 8
HALF_ROT = HEAD_DIM // 2
QK_SCALE = HEAD_DIM ** -0.5 * np.log2(np.e)

QKV_TILE = 1024
MLP_TILE = 1024
PART_ROWS = 256
ATT_TILE = 1024
ATT_Q_BLOCK = 2 * BAND
ATT_K_BLOCK = 4 * BAND
MAX_STRIDE = 4
ATT_BLOCKS_PER_BODY = (8, 4, 16)
VMEM_LIMIT = 60 * 1024 * 1024

F32 = jnp.float32
BF16 = jnp.bfloat16


def _resident(shape):
    return pl.BlockSpec(shape, lambda *_: (0,) * len(shape), pipeline_mode=pl.Buffered(1))


def _params(n_axes, semantics="parallel"):
    return pltpu.CompilerParams(
        dimension_semantics=(semantics,) * n_axes, vmem_limit_bytes=VMEM_LIMIT)


def _modulated_norm(x, gain, scale, shift):
    r = lax.rsqrt(jnp.mean(x * x, axis=-1, keepdims=True) + EPS)
    return (x * r) * (gain * (1.0 + scale)) + shift


def _ada_kernel(c_ref, w_ref, b_ref, o_ref):
    c = c_ref[...]
    s = c * jax.nn.sigmoid(c)
    o_ref[...] = jnp.dot(s, w_ref[...], precision=lax.Precision.HIGHEST,
                         preferred_element_type=F32) + b_ref[...]


def _ada(c_all, w_ada, b_ada):
    rows, _ = c_all.shape
    n_out = w_ada.shape[1]
    bn = n_out // 4
    return pl.pallas_call(
        _ada_kernel,
        grid=(n_out // bn,),
        in_specs=[
            pl.BlockSpec((rows, D_MODEL), lambda j: (0, 0)),
            pl.BlockSpec((D_MODEL, bn), lambda j: (0, j)),
            pl.BlockSpec((1, bn), lambda j: (0, j)),
        ],
        out_specs=pl.BlockSpec((rows, bn), lambda j: (0, j)),
        out_shape=jax.ShapeDtypeStruct((rows, n_out), F32),
        compiler_params=_params(1),
        name="ada",
    )(c_all, w_ada, b_ada)


def _qkv_kernel(x_ref, mod_ref, g1_ref, rope_base_ref, rope_tile_ref, w_ref, *rest):
    qkv_refs = rest[:3 * N_GROUPS]
    u_ref = rest[3 * N_GROUPS]
    scr_ref = rest[3 * N_GROUPS + 1]
    tm = x_ref.shape[1]
    mod = mod_ref[0]
    ta = rope_tile_ref[0]
    ca, sa, ssa = ta[0:1], ta[1:2], ta[2:3]
    lane = lax.broadcasted_iota(jnp.int32, (1, LANES), 1)
    first_half = (lane // HALF_ROT) % 2 == 0

    rp = PART_ROWS
    for part in range(tm // rp):
        rows = slice(part * rp, (part + 1) * rp)
        xn = _modulated_norm(x_ref[0, rows], g1_ref[...], mod[1:2], mod[0:1])
        xb = xn.astype(BF16)

        lhs = {1: xb}
        prev_dil, prev = 1, xn
        for k, dil in enumerate(DILS[1:]):
            step = dil // prev_dil
            assert dil % prev_dil == 0 and step <= MAX_STRIDE
            n = rp // dil
            for c in range(D_MODEL // LANES):
                scr_ref[part % 2, k, c] = prev[:, c * LANES:(c + 1) * LANES]
            prev = jnp.concatenate(
                [jnp.concatenate([scr_ref[part % 2, k, c,
                                          pl.ds((r % prev_dil) * (rp // prev_dil) + r // prev_dil, n, stride=step), :]
                                  for r in range(dil)], axis=0)
                 for c in range(D_MODEL // LANES)], axis=1)
            lhs[dil] = prev.astype(BF16)
            prev_dil = dil

        def emit(out_ref, dil, val, part=part, rp=rp):
            n = rp // dil
            for r in range(dil):
                out_ref[0, r, part * n:(part + 1) * n] = val[r * n:(r + 1) * n].astype(BF16)

        def rope(val, scale, g, rows=rows):
            cb, sb, ssb = rope_base_ref[g, 0, rows], rope_base_ref[g, 1, rows], rope_base_ref[g, 2, rows]
            cos = (ca * cb - sa * sb) * scale
            sin = (ssa * cb + ca * ssb) * scale
            parts = []
            for p in range(GROUP_WIDTH // LANES):
                xp = val[:, p * LANES:(p + 1) * LANES]
                partner = jnp.where(first_half, pltpu.roll(xp, LANES - HALF_ROT, 1),
                                    pltpu.roll(xp, HALF_ROT, 1))
                parts.append(xp * cos + partner * sin)
            return jnp.concatenate(parts, axis=1)

        for which in range(3):
            for g, dil in enumerate(DILS):
                col = which * ATT_WIDTH + g * GROUP_WIDTH
                val = jnp.dot(lhs[dil], w_ref[:, col:col + GROUP_WIDTH], preferred_element_type=F32)
                if which < 2:
                    val = rope(val, QK_SCALE if which == 0 else 1.0, g)
                emit(qkv_refs[3 * g + which], dil, val)

        col = 3 * ATT_WIDTH
        h = jnp.dot(xb, w_ref[:, col:col + D_MODEL], preferred_element_type=F32)
        cg = jnp.dot(xb, w_ref[:, col + D_MODEL:col + 2 * D_MODEL], preferred_element_type=F32)
        u_ref[0, rows] = cg * h


def _qkv(x, mod, g1, w_in):
    nb, seq, _ = x.shape
    tm = QKV_TILE
    n_cols = 3 * ATT_WIDTH + 2 * D_MODEL
    rope_base, rope_tile = _rope_tables(seq, tm)
    out_shapes = []
    out_specs = []
    for dil in DILS:
        for _ in range(3):
            out_shapes.append(jax.ShapeDtypeStruct((nb, dil, seq // dil, GROUP_WIDTH), BF16))
            out_specs.append(pl.BlockSpec((1, dil, tm // dil, GROUP_WIDTH), lambda b, i: (b, 0, i, 0)))
    out_shapes.append(jax.ShapeDtypeStruct((nb, seq, D_MODEL), F32))
    out_specs.append(pl.BlockSpec((1, tm, D_MODEL), lambda b, i: (b, i, 0)))
    return pl.pallas_call(
        _qkv_kernel,
        grid=(nb, seq // tm),
        in_specs=[
            pl.BlockSpec((1, tm, D_MODEL), lambda b, i: (b, i, 0)),
            pl.BlockSpec((1, 6, D_MODEL), lambda b, i: (b, 0, 0)),
            pl.BlockSpec((1, D_MODEL), lambda b, i: (0, 0)),
            _resident((N_GROUPS, 3, tm, LANES)),
            pl.BlockSpec((1, 3, LANES), lambda b, i: (i, 0, 0)),
            pl.BlockSpec((D_MODEL, n_cols), lambda b, i: (0, 0), pipeline_mode=pl.Buffered(1)),
        ],
        out_specs=out_specs,
        out_shape=out_shapes,
        scratch_shapes=[pltpu.VMEM((2, N_GROUPS - 1, D_MODEL // LANES, PART_ROWS, LANES), F32)],
        compiler_params=_params(2),
        name="qkv",
    )(x, mod, g1, rope_base, rope_tile, w_in)


def _attn_kernel(*refs):
    in_refs = refs[:7 * N_GROUPS]
    att_ref = refs[7 * N_GROUPS]
    (o_scr, lse_scr, kf_scr, vf_scr, st_scr, mt_scr, biast_scr,
     sr_scr, mr_scr, biasr_scr) = refs[7 * N_GROUPS + 1:]
    tm = att_ref.shape[1]
    i = pl.program_id(1)
    last = pl.num_programs(1) - 1

    n_pairs = GROUP_WIDTH // LANES
    lane = lax.broadcasted_iota(jnp.int32, (1, LANES), 1)
    head_a = lane < HEAD_DIM
    tk = ATT_K_BLOCK
    mask_forms = {}

    def build_masks():
        for (transposed, tq), bias_scr in mask_forms.items():
            if transposed:
                key = lax.broadcasted_iota(jnp.int32, (tk, tq), 0)
                qry = lax.broadcasted_iota(jnp.int32, (tk, tq), 1)
            else:
                qry = lax.broadcasted_iota(jnp.int32, (tq, tk), 0)
                key = lax.broadcasted_iota(jnp.int32, (tq, tk), 1)
            rel = key - BAND - qry
            band = (rel >= -BAND) & (rel <= BAND)
            after_start = key >= BAND
            before_end = key < tq + BAND
            for variant, valid in enumerate((band, band & after_start, band & before_end,
                                             band & after_start & before_end)):
                bias_scr[variant] = jnp.where(valid, 0.0, NEG).astype(F32)

    for g, dil in enumerate(DILS):
        tq = min(ATT_Q_BLOCK, tm // dil)
        mask_forms.setdefault((tq == LANES, tq), biast_scr if tq == LANES else biasr_scr)
    pl.when(jnp.logical_and(pl.program_id(0) == 0, i == 0))(build_masks)

    for g in sorted(range(N_GROUPS), key=lambda g: -DILS[g]):
        dil = DILS[g]
        q_ref, kp_ref, kc_ref, kn_ref, vp_ref, vc_ref, vn_ref = in_refs[7 * g:7 * g + 7]
        two_hops = dil > MAX_STRIDE
        if two_hops:
            assert dil % MAX_STRIDE == 0 and dil // MAX_STRIDE <= MAX_STRIDE and MAX_STRIDE in DILS
        store_g = DILS.index(MAX_STRIDE) if two_hops else g
        n = tm // dil
        tq = min(ATT_Q_BLOCK, n)
        nsub = n // tq
        per_body = ATT_BLOCKS_PER_BODY[g]
        n_iters = dil * nsub // per_body
        one_class = nsub >= per_body
        if one_class:
            assert dil == 1 and nsub % per_body == 0
        else:
            assert per_body % (2 * nsub) == 0 and dil % (per_body // nsub) == 0

        transposed = tq == LANES
        s_scr, m_scr, bias_scr = (st_scr, mt_scr, biast_scr) if transposed else (sr_scr, mr_scr, biasr_scr)

        def block_coords(it, j, nsub=nsub, per_body=per_body, one_class=one_class):
            if one_class:
                return 0, it * per_body + j, 0
            return it * (per_body // nsub) + j // nsub, j % nsub, (j // nsub) % 2

        def stage_kv(r, slot, n=n, tq=tq, kp_ref=kp_ref, kc_ref=kc_ref, kn_ref=kn_ref,
                     vp_ref=vp_ref, vc_ref=vc_ref, vn_ref=vn_ref):
            pad = tk - tq - 2 * BAND
            for full, before, cur, after in ((kf_scr, kp_ref, kc_ref, kn_ref),
                                             (vf_scr, vp_ref, vc_ref, vn_ref)):
                full[slot, 0:BAND] = before[0, r]
                full[slot, BAND:BAND + n] = cur[0, r]
                full[slot, BAND + n:2 * BAND + n] = after[0, r]
                if pad:
                    full[slot, 2 * BAND + n:2 * BAND + n + pad] = jnp.zeros((pad, GROUP_WIDTH), full.dtype)

        def edge_bias(sub, nsub=nsub, bias_scr=bias_scr):
            first = jnp.logical_and(i == 0, sub == 0)
            final = jnp.logical_and(i == last, sub == nsub - 1)
            return bias_scr[first.astype(jnp.int32) + 2 * final.astype(jnp.int32)]

        def scores(it, j, tq=tq, q_ref=q_ref, block_coords=block_coords, stage_kv=stage_kv,
                   one_class=one_class, edge_bias=edge_bias, transposed=transposed,
                   s_scr=s_scr, m_scr=m_scr):
            r, sub, kv_slot = block_coords(it, j)
            if not one_class and sub == 0:
                stage_kv(r, kv_slot)
            q0 = sub * tq if isinstance(sub, int) else pl.multiple_of(sub * tq, tq)
            bias = edge_bias(sub)
            for p in range(n_pairs):
                cols = slice(p * LANES, (p + 1) * LANES)
                qp = q_ref[0, r, pl.ds(q0, tq), cols]
                kp = kf_scr[kv_slot, pl.ds(q0, tk), cols]
                zeros = jnp.zeros_like(qp)
                q_heads = (jnp.where(head_a, qp, zeros), jnp.where(head_a, zeros, qp))
                if transposed:
                    s = lax.dot_general(kp, jnp.concatenate(q_heads, axis=0),
                                        (((1,), (1,)), ((), ())), preferred_element_type=F32)
                    s = jnp.concatenate([s[:, 0:tq] + bias, s[:, tq:2 * tq] + bias], axis=1)
                    s_scr[j % 2, p] = s
                    m_scr[j % 2, p] = jnp.broadcast_to(jnp.max(s, axis=0, keepdims=True),
                                                       (SUBLANES, 2 * tq))
                else:
                    s = lax.dot_general(jnp.concatenate(q_heads, axis=0), kp, (((1,), (1,)), ((), ())),
                                        preferred_element_type=F32)
                    s = s + jnp.concatenate([bias, bias], axis=0)
                    s_scr[j % 2, p] = s
                    m_scr[j % 2, p] = jnp.broadcast_to(jnp.max(s, axis=-1, keepdims=True),
                                                       (2 * tq, LANES))

        def finish(it, j, tq=tq, dil=dil, g=g, block_coords=block_coords, transposed=transposed,
                   s_scr=s_scr, m_scr=m_scr):
            r, sub, kv_slot = block_coords(it, j)
            q0 = sub * tq if isinstance(sub, int) else pl.multiple_of(sub * tq, tq)
            for p in range(n_pairs):
                cols = slice(p * LANES, (p + 1) * LANES)
                vp = vf_scr[kv_slot, pl.ds(q0, tk), cols]
                if transposed:
                    vt = vp.T
                    ones = jnp.ones((2 * SUBLANES, tk), BF16)
                    m = m_scr[j % 2, p]
                    e = jnp.exp2(s_scr[j % 2, p] - m[0:1]).astype(BF16)
                    tops, lses = [], []
                    for hh in range(2):
                        lhs = jnp.concatenate([vt[hh * HEAD_DIM:(hh + 1) * HEAD_DIM], ones], axis=0)
                        res = jnp.dot(lhs, e[:, hh * tq:(hh + 1) * tq], preferred_element_type=F32)
                        den = res[HEAD_DIM:HEAD_DIM + SUBLANES]
                        tops.append(res[0:HEAD_DIM] * jnp.tile(1.0 / den, (HEAD_DIM // SUBLANES, 1)))
                        lse = m[:, hh * tq:(hh + 1) * tq] + jnp.log2(den)
                        lses.append(jnp.tile(lse, (HEAD_DIM // SUBLANES, 1)))
                    o2 = jnp.concatenate(tops, axis=0).T
                    lse2 = jnp.concatenate(lses, axis=0).T
                else:
                    s = s_scr[j % 2, p]
                    m = m_scr[j % 2, p]
                    e = jnp.concatenate([jnp.exp2(s[:, t * LANES:(t + 1) * LANES] - m)
                                         for t in range(tk // LANES)], axis=1)
                    rhs = jnp.concatenate([vp, jnp.ones_like(vp)], axis=1)
                    res = jnp.dot(e.astype(BF16), rhs, preferred_element_type=F32)
                    den = jnp.where(head_a, res[0:tq, LANES:2 * LANES], res[tq:2 * tq, LANES:2 * LANES])
                    o2 = jnp.where(head_a, res[0:tq, 0:LANES], res[tq:2 * tq, 0:LANES]) / den
                    lse2 = jnp.where(head_a, m[0:tq], m[tq:2 * tq]) + jnp.log2(den)
                if dil == 1:
                    rows = pl.ds(q0, tq)
                elif two_hops:
                    inner = dil // MAX_STRIDE
                    rows = pl.ds((r % MAX_STRIDE) * (tm // MAX_STRIDE) + r // MAX_STRIDE + inner * q0,
                                 tq, stride=inner)
                else:
                    rows = pl.ds(r + dil * q0, tq, stride=dil)
                o_scr[store_g * n_pairs + p, rows, :] = o2
                lse_scr[store_g * n_pairs + p, rows, :] = lse2

        if one_class:
            stage_kv(0, 0)
        scores(0, 0)

        def body(it, carry, per_body=per_body, n_iters=n_iters, scores=scores, finish=finish):
            for j in range(per_body):
                if j + 1 < per_body:
                    scores(it, j + 1)
                else:
                    scores(jnp.minimum(it + 1, n_iters - 1), 0)
                finish(it, j)
            return carry

        lax.fori_loop(0, n_iters, body, 0)

        if two_hops:
            def second_hop(r2, carry, g=g, store_g=store_g):
                per_class = tm // MAX_STRIDE
                for scr in (o_scr, lse_scr):
                    for p in range(n_pairs):
                        for c in range(per_class // LANES):
                            src = pl.ds(pl.multiple_of(r2 * per_class, per_class) + c * LANES, LANES)
                            dst = pl.ds(r2 + MAX_STRIDE * c * LANES, LANES, stride=MAX_STRIDE)
                            scr[g * n_pairs + p, dst, :] = scr[store_g * n_pairs + p, src, :]
                return carry

            lax.fori_loop(0, MAX_STRIDE, second_hop, 0)

    chunk = 128

    def merge_body(c, carry):
        rows = pl.ds(pl.multiple_of(c * chunk, chunk), chunk)
        for p in range(n_pairs):
            ls = [lse_scr[g * n_pairs + p, rows, :] for g in range(N_GROUPS)]
            mx = functools.reduce(jnp.maximum, ls)
            ws = [jnp.exp2(l - mx) for l in ls]
            num = sum(w * o_scr[g * n_pairs + p, rows, :] for g, w in enumerate(ws))
            att_ref[0, rows, p * LANES:(p + 1) * LANES] = (num / sum(ws)).astype(att_ref.dtype)
        return carry

    lax.fori_loop(0, tm // chunk, merge_body, 0)


def _attn(qkv):
    nb = qkv[0].shape[0]
    seq = qkv[0].shape[1] * qkv[0].shape[2]
    tm = ATT_TILE
    operands = []
    in_specs = []
    kv_rows = 0
    for g, dil in enumerate(DILS):
        q, k, v = qkv[3 * g:3 * g + 3]
        n = tm // dil
        nblk = n // BAND
        n_halo = seq // dil // BAND
        cur = pl.BlockSpec((1, dil, n, GROUP_WIDTH), lambda b, i: (b, 0, i, 0))
        prev = pl.BlockSpec((1, dil, BAND, GROUP_WIDTH),
                            lambda b, i, nblk=nblk: (b, 0, jnp.maximum(i * nblk - 1, 0), 0))
        nxt = pl.BlockSpec((1, dil, BAND, GROUP_WIDTH),
                           lambda b, i, nblk=nblk, n_halo=n_halo: (b, 0, jnp.minimum((i + 1) * nblk, n_halo - 1), 0))
        operands += [q, k, k, k, v, v, v]
        in_specs += [cur, prev, cur, nxt, prev, cur, nxt]
        kv_rows = max(kv_rows, n - min(ATT_Q_BLOCK, n) + ATT_K_BLOCK)
    merge_shape = (N_GROUPS * GROUP_WIDTH // LANES, tm, LANES)
    tq_small = min(ATT_Q_BLOCK, tm // max(DILS))
    scratch = [
        pltpu.VMEM(merge_shape, F32),
        pltpu.VMEM(merge_shape, F32),
        pltpu.VMEM((2, kv_rows, GROUP_WIDTH), BF16),
        pltpu.VMEM((2, kv_rows, GROUP_WIDTH), BF16),
        pltpu.VMEM((2, GROUP_WIDTH // LANES, ATT_K_BLOCK, 2 * ATT_Q_BLOCK), F32),
        pltpu.VMEM((2, GROUP_WIDTH // LANES, SUBLANES, 2 * ATT_Q_BLOCK), F32),
        pltpu.VMEM((4, ATT_K_BLOCK, ATT_Q_BLOCK), F32),
        pltpu.VMEM((2, GROUP_WIDTH // LANES, 2 * tq_small, ATT_K_BLOCK), F32),
        pltpu.VMEM((2, GROUP_WIDTH // LANES, 2 * tq_small, LANES), F32),
        pltpu.VMEM((4, tq_small, ATT_K_BLOCK), F32),
    ]
    return pl.pallas_call(
        _attn_kernel,
        grid=(nb, seq // tm),
        in_specs=in_specs,
        out_specs=pl.BlockSpec((1, tm, GROUP_WIDTH), lambda b, i: (b, i, 0)),
        out_shape=jax.ShapeDtypeStruct((nb, seq, GROUP_WIDTH), BF16),
        scratch_shapes=scratch,
        compiler_params=_params(2, "arbitrary"),
        name="attn",
    )(*operands)


def _mix_kernel(x_ref, att_ref, u_ref, up_ref, un_ref, mod_ref, g1_ref, cw_ref, cb_ref,
                w3_ref, wa_ref, wc_ref, wm_ref, o_ref):
    tm = x_ref.shape[1]
    i = pl.program_id(1)
    last = pl.num_programs(1) - 1
    mod = mod_ref[0]
    halo = up_ref.shape[1]
    cw = cw_ref[...]
    rp = PART_ROWS
    n_parts = tm // rp
    row_id = lax.broadcasted_iota(jnp.int32, (rp, 1), 0)

    for part in range(n_parts):
        rows = slice(part * rp, (part + 1) * rp)
        x = x_ref[0, rows]
        xb = _modulated_norm(x, g1_ref[...], mod[1:2], mod[0:1]).astype(BF16)

        u = u_ref[0, rows]
        if part == 0:
            u_before = jnp.where(i > 0, up_ref[0, halo - 1:halo, :], 0.0)
        else:
            u_before = u_ref[0, part * rp - 1:part * rp, :]
        if part == n_parts - 1:
            u_after = jnp.where(i < last, un_ref[0, 0:1, :], 0.0)
        else:
            u_after = u_ref[0, (part + 1) * rp:(part + 1) * rp + 1, :]
        u_m1 = jnp.where(row_id == 0, u_before, pltpu.roll(u, 1, 0))
        u_p1 = jnp.where(row_id == rp - 1, u_after, pltpu.roll(u, rp - 1, 0))
        conv = cb_ref[...] + u_m1 * cw[0:1] + u * cw[1:2] + u_p1 * cw[2:3]

        b_gate = jnp.dot(xb, w3_ref[:, 0:D_MODEL], preferred_element_type=F32)
        cv = jnp.dot((b_gate * conv).astype(BF16), wc_ref[...], preferred_element_type=F32)
        g_att = jnp.dot(xb, w3_ref[:, D_MODEL:2 * D_MODEL], preferred_element_type=F32)
        ap = jnp.dot(att_ref[0, rows], wa_ref[...], preferred_element_type=F32)
        merged = jax.nn.sigmoid(g_att) * ap
        g_conv = jnp.dot(xb, w3_ref[:, 2 * D_MODEL:3 * D_MODEL], preferred_element_type=F32)
        merged = merged + jax.nn.sigmoid(g_conv) * cv
        mix = jnp.dot(merged.astype(BF16), wm_ref[...], preferred_element_type=F32)
        o_ref[0, rows] = x + mod[2:3] * mix


def _mix(x, att, u, mod, g1, conv_w, conv_b, w_bgg, w_att, w_conv, w_mixo):
    nb, seq, _ = x.shape
    tm = MLP_TILE
    halo = 8
    nh = tm // halo
    return pl.pallas_call(
        _mix_kernel,
        grid=(nb, seq // tm),
        in_specs=[
            pl.BlockSpec((1, tm, D_MODEL), lambda b, i: (b, i, 0)),
            pl.BlockSpec((1, tm, GROUP_WIDTH), lambda b, i: (b, i, 0)),
            pl.BlockSpec((1, tm, D_MODEL), lambda b, i: (b, i, 0)),
            pl.BlockSpec((1, halo, D_MODEL), lambda b, i: (b, jnp.maximum(i * nh - 1, 0), 0)),
            pl.BlockSpec((1, halo, D_MODEL), lambda b, i: (b, jnp.minimum((i + 1) * nh, seq // halo - 1), 0)),
            pl.BlockSpec((1, 6, D_MODEL), lambda b, i: (b, 0, 0)),
            pl.BlockSpec((1, D_MODEL), lambda b, i: (0, 0)),
            pl.BlockSpec((CONV_WIDTH, D_MODEL), lambda b, i: (0, 0)),
            pl.BlockSpec((1, D_MODEL), lambda b, i: (0, 0)),
            _resident(w_bgg.shape),
            _resident(w_att.shape),
            _resident(w_conv.shape),
            _resident(w_mixo.shape),
        ],
        out_specs=pl.BlockSpec((1, tm, D_MODEL), lambda b, i: (b, i, 0)),
        out_shape=jax.ShapeDtypeStruct((nb, seq, D_MODEL), F32),
        compiler_params=_params(2),
        name="mix",
    )(x, att, u, u, u, mod, g1, conv_w, conv_b, w_bgg, w_att, w_conv, w_mixo)


def _mlp_kernel(x_ref, mod_ref, g2_ref, gf_ref, w1_ref, w2_ref, o_ref):
    mod = mod_ref[0]
    tm = x_ref.shape[1]
    rp = PART_ROWS
    ff_chunk = D_MODEL
    for part in range(tm // rp):
        rows = slice(part * rp, (part + 1) * rp)
        x = x_ref[0, rows]
        xb = _modulated_norm(x, g2_ref[...], mod[4:5], mod[3:4]).astype(BF16)
        acc = jnp.zeros(x.shape, F32)
        for c in range(D_FF // ff_chunk):
            cols = slice(c * ff_chunk, (c + 1) * ff_chunk)
            h = jnp.dot(xb, w1_ref[:, cols], preferred_element_type=F32)
            h = jnp.square(jnp.maximum(h, 0.0)).astype(BF16)
            acc = acc + jnp.dot(h, w2_ref[cols, :], preferred_element_type=F32)
        x2 = x + mod[5:6] * acc
        r = lax.rsqrt(jnp.mean(x2 * x2, axis=-1, keepdims=True) + EPS)
        o_ref[0, rows] = (x2 * r) * gf_ref[...]


def _mlp(x1, mod, g2, gf, w1, w2):
    nb, seq, _ = x1.shape
    tm = MLP_TILE
    return pl.pallas_call(
        _mlp_kernel,
        grid=(nb, seq // tm),
        in_specs=[
            pl.BlockSpec((1, tm, D_MODEL), lambda b, i: (b, i, 0)),
            pl.BlockSpec((1, 6, D_MODEL), lambda b, i: (b, 0, 0)),
            pl.BlockSpec((1, D_MODEL), lambda b, i: (0, 0)),
            pl.BlockSpec((1, D_MODEL), lambda b, i: (0, 0)),
            _resident(w1.shape),
            _resident(w2.shape),
        ],
        out_specs=pl.BlockSpec((1, tm, D_MODEL), lambda b, i: (b, i, 0)),
        out_shape=jax.ShapeDtypeStruct((nb, seq, D_MODEL), F32),
        compiler_params=_params(2),
        name="mlp",
    )(x1, mod, g2, gf, w1, w2)


def _rope_tables(seq, tm):
    inv = 1.0 / (ROPE_THETA ** (np.arange(HALF_ROT, dtype=np.float64) / HALF_ROT))
    inv = np.tile(inv, LANES // HALF_ROT)[None, :]
    sign = np.where((np.arange(LANES) // HALF_ROT) % 2 == 0, -1.0, 1.0)[None, :]
    offsets = []
    for dil in DILS:
        o = np.arange(tm).reshape(tm // PART_ROWS, PART_ROWS // dil, dil)
        offsets.append(o.transpose(0, 2, 1).reshape(tm))
    ang_b = np.stack(offsets).astype(np.float64)[:, :, None] * inv[None]
    ang_a = (np.arange(seq // tm, dtype=np.float64) * tm)[:, None] * inv
    base = np.stack([np.cos(ang_b), np.sin(ang_b), sign * np.sin(ang_b)], axis=1)
    tile = np.stack([np.cos(ang_a), np.sin(ang_a), sign * np.sin(ang_a)], axis=1)
    return jnp.asarray(base, F32), jnp.asarray(tile, F32)


def _encoder(x, mod, g1, g2, gf, conv_w, conv_b, w_in, w_bgg, w_att, w_conv, w_mixo, w1, w2):
    outs = _qkv(x, mod, g1, w_in)
    att = _attn(outs[:3 * N_GROUPS])
    u = outs[3 * N_GROUPS]
    x1 = _mix(x, att, u, mod, g1, conv_w, conv_b, w_bgg, w_att, w_conv, w_mixo)
    return _mlp(x1, mod, g2, gf, w1, w2)


def kernel(x_prompt, x_sample, c_prompt, c_sample, w_ada, b_ada, norm1_g, w_in, conv_w, conv_b,
           w_attn_out, w_conv_out, w_mix_out, norm2_g, w_mlp_in, w_mlp_out, final_norm_g):
    assert w_ada.shape[0] == 1, "single layer"
    nb_p, nb_s = x_prompt.shape[0], x_sample.shape[0]
    pad = -(nb_p + nb_s) % 8
    c_all = jnp.concatenate([c_prompt, c_sample, jnp.zeros((pad, D_MODEL), F32)], axis=0)
    mod = _ada(c_all, w_ada[0], b_ada[0][None, :])
    mod = mod.reshape(mod.shape[0], 6, D_MODEL)

    w_in_b = w_in[0].astype(BF16)
    w_bgg = w_in_b[:, 3 * ATT_WIDTH + 2 * D_MODEL:]
    shared = (norm1_g, norm2_g, final_norm_g[None, :], conv_w[0], conv_b, w_in_b, w_bgg,
              w_attn_out[0].astype(BF16), w_conv_out[0].astype(BF16), w_mix_out[0].astype(BF16),
              w_mlp_in[0].astype(BF16), w_mlp_out[0].astype(BF16))

    ys = []
    for x, m in ((x_prompt, mod[:nb_p]), (x_sample, mod[nb_p:nb_p + nb_s])):
        ys.append(_encoder(x, m, *shared))
    return tuple(ys)
```

```python
import functools

import jax
import jax.numpy as jnp
import numpy as np
from jax import lax
from jax.experimental import pallas as pl
from jax.experimental.pallas import tpu as pltpu

D_MODEL = 1024
HEAD_DIM = 64
HEADS_PER_GROUP = 8
GROUP_WIDTH = HEADS_PER_GROUP * HEAD_DIM
DILATION_GROUPS = ((128, 1), (512, 4), (2048, 16))
DILS = tuple(d for _, d in DILATION_GROUPS)
BAND = DILATION_GROUPS[0][0] // (2 * DILATION_GROUPS[0][1])
assert all(w // (2 * d) == BAND for w, d in DILATION_GROUPS)
N_GROUPS = len(DILS)
ATT_WIDTH = N_GROUPS * GROUP_WIDTH
CONV_WIDTH = 3
D_FF = 4 * D_MODEL
ROPE_THETA = 10000.0
EPS = 1e-6
NEG = -1e30

LANES = 128
SUBLANES = 8
HALF_ROT = HEAD_DIM // 2
QK_SCALE = HEAD_DIM ** -0.5 * np.log2(np.e)

QKV_TILE = 1024
MLP_TILE = 1024
PART_ROWS = 256
ATT_TILE = 1024
ATT_Q_BLOCK = 2 * BAND
ATT_K_BLOCK = 4 * BAND
MAX_STRIDE = 4
ATT_BLOCKS_PER_BODY = (8, 4, 16)
VMEM_LIMIT = 60 * 1024 * 1024

F32 = jnp.float32
BF16 = jnp.bfloat16


def _resident(shape):
    return pl.BlockSpec(shape, lambda *_: (0,) * len(shape), pipeline_mode=pl.Buffered(1))


def _params(n_axes, semantics="parallel"):
    return pltpu.CompilerParams(
        dimension_semantics=(semantics,) * n_axes, vmem_limit_bytes=VMEM_LIMIT)


def _modulated_norm(x, gain, scale, shift):
    r = lax.rsqrt(jnp.mean(x * x, axis=-1, keepdims=True) + EPS)
    return (x * r) * (gain * (1.0 + scale)) + shift


def _ada_kernel(c_ref, w_ref, b_ref, o_ref):
    c = c_ref[...]
    s = c * jax.nn.sigmoid(c)
    o_ref[...] = jnp.dot(s, w_ref[...], precision=lax.Precision.HIGHEST,
                         preferred_element_type=F32) + b_ref[...]


def _ada(c_all, w_ada, b_ada):
    rows, _ = c_all.shape
    n_out = w_ada.shape[1]
    bn = n_out // 4
    return pl.pallas_call(
        _ada_kernel,
        grid=(n_out // bn,),
        in_specs=[
            pl.BlockSpec((rows, D_MODEL), lambda j: (0, 0)),
            pl.BlockSpec((D_MODEL, bn), lambda j: (0, j)),
            pl.BlockSpec((1, bn), lambda j: (0, j)),
        ],
        out_specs=pl.BlockSpec((rows, bn), lambda j: (0, j)),
        out_shape=jax.ShapeDtypeStruct((rows, n_out), F32),
        compiler_params=_params(1),
        name="ada",
    )(c_all, w_ada, b_ada)


def _qkv_kernel(x_ref, mod_ref, g1_ref, rope_base_ref, rope_tile_ref, w_ref, *rest):
    qkv_refs = rest[:3 * N_GROUPS]
    u_ref = rest[3 * N_GROUPS]
    scr_ref = rest[3 * N_GROUPS + 1]
    tm = x_ref.shape[1]
    mod = mod_ref[0]
    ta = rope_tile_ref[0]
    ca, sa, ssa = ta[0:1], ta[1:2], ta[2:3]
    lane = lax.broadcasted_iota(jnp.int32, (1, LANES), 1)
    first_half = (lane // HALF_ROT) % 2 == 0

    rp = PART_ROWS
    for part in range(tm // rp):
        rows = slice(part * rp, (part + 1) * rp)
        xn = _modulated_norm(x_ref[0, rows], g1_ref[...], mod[1:2], mod[0:1])
        xb = xn.astype(BF16)

        lhs = {1: xb}
        prev_dil, prev = 1, xn
        for k, dil in enumerate(DILS[1:]):
            step = dil // prev_dil
            assert dil % prev_dil == 0 and step <= MAX_STRIDE
            n = rp // dil
            for c in range(D_MODEL // LANES):
                scr_ref[part % 2, k, c] = prev[:, c * LANES:(c + 1) * LANES]
            prev = jnp.concatenate(
                [jnp.concatenate([scr_ref[part % 2, k, c,
                                          pl.ds((r % prev_dil) * (rp // prev_dil) + r // prev_dil, n, stride=step), :]
                                  for r in range(dil)], axis=0)
                 for c in range(D_MODEL // LANES)], axis=1)
            lhs[dil] = prev.astype(BF16)
            prev_dil = dil

        def emit(out_ref, dil, val, part=part, rp=rp):
            n = rp // dil
            for r in range(dil):
                out_ref[0, r, part * n:(part + 1) * n] = val[r * n:(r + 1) * n].astype(BF16)

        def rope(val, scale, g, rows=rows):
            cb, sb, ssb = rope_base_ref[g, 0, rows], rope_base_ref[g, 1, rows], rope_base_ref[g, 2, rows]
            cos = (ca * cb - sa * sb) * scale
            sin = (ssa * cb + ca * ssb) * scale
            parts = []
            for p in range(GROUP_WIDTH // LANES):
                xp = val[:, p * LANES:(p + 1) * LANES]
                partner = jnp.where(first_half, pltpu.roll(xp, LANES - HALF_ROT, 1),
                                    pltpu.roll(xp, HALF_ROT, 1))
                parts.append(xp * cos + partner * sin)
            return jnp.concatenate(parts, axis=1)

        for which in range(3):
            for g, dil in enumerate(DILS):
                col = which * ATT_WIDTH + g * GROUP_WIDTH
                val = jnp.dot(lhs[dil], w_ref[:, col:col + GROUP_WIDTH], preferred_element_type=F32)
                if which < 2:
                    val = rope(val, QK_SCALE if which == 0 else 1.0, g)
                emit(qkv_refs[3 * g + which], dil, val)

        col = 3 * ATT_WIDTH
        h = jnp.dot(xb, w_ref[:, col:col + D_MODEL], preferred_element_type=F32)
        cg = jnp.dot(xb, w_ref[:, col + D_MODEL:col + 2 * D_MODEL], preferred_element_type=F32)
        u_ref[0, rows] = cg * h


def _qkv(x, mod, g1, w_in):
    nb, seq, _ = x.shape
    tm = QKV_TILE
    n_cols = 3 * ATT_WIDTH + 2 * D_MODEL
    rope_base, rope_tile = _rope_tables(seq, tm)
    out_shapes = []
    out_specs = []
    for dil in DILS:
        for _ in range(3):
            out_shapes.append(jax.ShapeDtypeStruct((nb, dil, seq // dil, GROUP_WIDTH), BF16))
            out_specs.append(pl.BlockSpec((1, dil, tm // dil, GROUP_WIDTH), lambda b, i: (b, 0, i, 0)))
    out_shapes.append(jax.ShapeDtypeStruct((nb, seq, D_MODEL), F32))
    out_specs.append(pl.BlockSpec((1, tm, D_MODEL), lambda b, i: (b, i, 0)))
    return pl.pallas_call(
        _qkv_kernel,
        grid=(nb, seq // tm),
        in_specs=[
            pl.BlockSpec((1, tm, D_MODEL), lambda b, i: (b, i, 0)),
            pl.BlockSpec((1, 6, D_MODEL), lambda b, i: (b, 0, 0)),
            pl.BlockSpec((1, D_MODEL), lambda b, i: (0, 0)),
            _resident((N_GROUPS, 3, tm, LANES)),
            pl.BlockSpec((1, 3, LANES), lambda b, i: (i, 0, 0)),
            pl.BlockSpec((D_MODEL, n_cols), lambda b, i: (0, 0), pipeline_mode=pl.Buffered(1)),
        ],
        out_specs=out_specs,
        out_shape=out_shapes,
        scratch_shapes=[pltpu.VMEM((2, N_GROUPS - 1, D_MODEL // LANES, PART_ROWS, LANES), F32)],
        compiler_params=_params(2),
        name="qkv",
    )(x, mod, g1, rope_base, rope_tile, w_in)


def _attn_kernel(*refs):
    in_refs = refs[:7 * N_GROUPS]
    att_ref = refs[7 * N_GROUPS]
    (o_scr, lse_scr, kf_scr, vf_scr, st_scr, mt_scr, biast_scr,
     sr_scr, mr_scr, biasr_scr) = refs[7 * N_GROUPS + 1:]
    tm = att_ref.shape[1]
    i = pl.program_id(1)
    last = pl.num_programs(1) - 1

    n_pairs = GROUP_WIDTH // LANES
    lane = lax.broadcasted_iota(jnp.int32, (1, LANES), 1)
    head_a = lane < HEAD_DIM
    tk = ATT_K_BLOCK
    mask_forms = {}

    def build_masks():
        for (transposed, tq), bias_scr in mask_forms.items():
            if transposed:
                key = lax.broadcasted_iota(jnp.int32, (tk, tq), 0)
                qry = lax.broadcasted_iota(jnp.int32, (tk, tq), 1)
            else:
                qry = lax.broadcasted_iota(jnp.int32, (tq, tk), 0)
                key = lax.broadcasted_iota(jnp.int32, (tq, tk), 1)
            rel = key - BAND - qry
            band = (rel >= -BAND) & (rel <= BAND)
            after_start = key >= BAND
            before_end = key < tq + BAND
            for variant, valid in enumerate((band, band & after_start, band & before_end,
                                             band & after_start & before_end)):
                bias_scr[variant] = jnp.where(valid, 0.0, NEG).astype(F32)

    for g, dil in enumerate(DILS):
        tq = min(ATT_Q_BLOCK, tm // dil)
        mask_forms.setdefault((tq == LANES, tq), biast_scr if tq == LANES else biasr_scr)
    pl.when(jnp.logical_and(pl.program_id(0) == 0, i == 0))(build_masks)

    assert DILS[0] == 1 and min(DILS[1:]) > 1
    for g in sorted(range(N_GROUPS), key=lambda g: -DILS[g]):
        dil = DILS[g]
        q_ref, kp_ref, kc_ref, kn_ref, vp_ref, vc_ref, vn_ref = in_refs[7 * g:7 * g + 7]
        two_hops = dil > MAX_STRIDE
        if two_hops:
            assert dil % MAX_STRIDE == 0 and dil // MAX_STRIDE <= MAX_STRIDE and MAX_STRIDE in DILS
        store_g = DILS.index(MAX_STRIDE) if two_hops else g
        n = tm // dil
        tq = min(ATT_Q_BLOCK, n)
        nsub = n // tq
        per_body = ATT_BLOCKS_PER_BODY[g]
        n_iters = dil * nsub // per_body
        one_class = nsub >= per_body
        if one_class:
            assert dil == 1 and nsub % per_body == 0
        else:
            assert per_body % (2 * nsub) == 0 and dil % (per_body // nsub) == 0

        transposed = tq == LANES
        s_scr, m_scr, bias_scr = (st_scr, mt_scr, biast_scr) if transposed else (sr_scr, mr_scr, biasr_scr)

        def block_coords(it, j, nsub=nsub, per_body=per_body, one_class=one_class):
            if one_class:
                return 0, it * per_body + j, 0
            return it * (per_body // nsub) + j // nsub, j % nsub, (j // nsub) % 2

        def stage_kv(r, slot, n=n, tq=tq, kp_ref=kp_ref, kc_ref=kc_ref, kn_ref=kn_ref,
                     vp_ref=vp_ref, vc_ref=vc_ref, vn_ref=vn_ref):
            pad = tk - tq - 2 * BAND
            for full, before, cur, after in ((kf_scr, kp_ref, kc_ref, kn_ref),
                                             (vf_scr, vp_ref, vc_ref, vn_ref)):
                full[slot, 0:BAND] = before[0, r]
                full[slot, BAND:BAND + n] = cur[0, r]
                full[slot, BAND + n:2 * BAND + n] = after[0, r]
                if pad:
                    full[slot, 2 * BAND + n:2 * BAND + n + pad] = jnp.zeros((pad, GROUP_WIDTH), full.dtype)

        def edge_bias(sub, nsub=nsub, bias_scr=bias_scr):
            first = jnp.logical_and(i == 0, sub == 0)
            final = jnp.logical_and(i == last, sub == nsub - 1)
            return bias_scr[first.astype(jnp.int32) + 2 * final.astype(jnp.int32)]

        def scores(it, j, tq=tq, q_ref=q_ref, block_coords=block_coords, stage_kv=stage_kv,
                   one_class=one_class, edge_bias=edge_bias, transposed=transposed,
                   s_scr=s_scr, m_scr=m_scr):
            r, sub, kv_slot = block_coords(it, j)
            if not one_class and sub == 0:
                stage_kv(r, kv_slot)
            q0 = sub * tq if isinstance(sub, int) else pl.multiple_of(sub * tq, tq)
            bias = edge_bias(sub)
            for p in range(n_pairs):
                cols = slice(p * LANES, (p + 1) * LANES)
                qp = q_ref[0, r, pl.ds(q0, tq), cols]
                kp = kf_scr[kv_slot, pl.ds(q0, tk), cols]
                zeros = jnp.zeros_like(qp)
                q_heads = (jnp.where(head_a, qp, zeros), jnp.where(head_a, zeros, qp))
                if transposed:
                    s = lax.dot_general(kp, jnp.concatenate(q_heads, axis=0),
                                        (((1,), (1,)), ((), ())), preferred_element_type=F32)
                    s = jnp.concatenate([s[:, 0:tq] + bias, s[:, tq:2 * tq] + bias], axis=1)
                    s_scr[j % 2, p] = s
                    m_scr[j % 2, p] = jnp.broadcast_to(jnp.max(s, axis=0, keepdims=True),
                                                       (SUBLANES, 2 * tq))
                else:
                    s = lax.dot_general(jnp.concatenate(q_heads, axis=0), kp, (((1,), (1,)), ((), ())),
                                        preferred_element_type=F32)
                    s = s + jnp.concatenate([bias, bias], axis=0)
                    s_scr[j % 2, p] = s
                    m_scr[j % 2, p] = jnp.broadcast_to(jnp.max(s, axis=-1, keepdims=True),
                                                       (2 * tq, LANES))

        def finish(it, j, tq=tq, dil=dil, g=g, block_coords=block_coords, transposed=transposed,
                   s_scr=s_scr, m_scr=m_scr):
            r, sub, kv_slot = block_coords(it, j)
            q0 = sub * tq if isinstance(sub, int) else pl.multiple_of(sub * tq, tq)
            for p in range(n_pairs):
                cols = slice(p * LANES, (p + 1) * LANES)
                vp = vf_scr[kv_slot, pl.ds(q0, tk), cols]
                if transposed:
                    vt = vp.T
                    ones = jnp.ones((2 * SUBLANES, tk), BF16)
                    m = m_scr[j % 2, p]
                    e = jnp.exp2(s_scr[j % 2, p] - m[0:1]).astype(BF16)
                    tops, lses = [], []
                    for hh in range(2):
                        lhs = jnp.concatenate([vt[hh * HEAD_DIM:(hh + 1) * HEAD_DIM], ones], axis=0)
                        res = jnp.dot(lhs, e[:, hh * tq:(hh + 1) * tq], preferred_element_type=F32)
                        den = res[HEAD_DIM:HEAD_DIM + SUBLANES]
                        tops.append(res[0:HEAD_DIM] * jnp.tile(1.0 / den, (HEAD_DIM // SUBLANES, 1)))
                        lse = m[:, hh * tq:(hh + 1) * tq] + jnp.log2(den)
                        lses.append(jnp.tile(lse, (HEAD_DIM // SUBLANES, 1)))
                    o2 = jnp.concatenate(tops, axis=0).T
                    lse2 = jnp.concatenate(lses, axis=0).T
                else:
                    s = s_scr[j % 2, p]
                    m = m_scr[j % 2, p]
                    e = jnp.concatenate([jnp.exp2(s[:, t * LANES:(t + 1) * LANES] - m)
                                         for t in range(tk // LANES)], axis=1)
                    rhs = jnp.concatenate([vp, jnp.ones_like(vp)], axis=1)
                    res = jnp.dot(e.astype(BF16), rhs, preferred_element_type=F32)
                    den = jnp.where(head_a, res[0:tq, LANES:2 * LANES], res[tq:2 * tq, LANES:2 * LANES])
                    o2 = jnp.where(head_a, res[0:tq, 0:LANES], res[tq:2 * tq, 0:LANES]) / den
                    lse2 = jnp.where(head_a, m[0:tq], m[tq:2 * tq]) + jnp.log2(den)
                if dil == 1:
                    rows = pl.ds(q0, tq)
                elif two_hops:
                    inner = dil // MAX_STRIDE
                    rows = pl.ds((r % MAX_STRIDE) * (tm // MAX_STRIDE) + r // MAX_STRIDE + inner * q0,
                                 tq, stride=inner)
                else:
                    rows = pl.ds(r + dil * q0, tq, stride=dil)
                if dil == 1:
                    ls = [lse2] + [lse_scr[s * n_pairs + p, rows, :] for s in range(N_GROUPS - 1)]
                    os = [o2] + [o_scr[s * n_pairs + p, rows, :] for s in range(N_GROUPS - 1)]
                    mx = functools.reduce(jnp.maximum, ls)
                    ws = [jnp.exp2(l - mx) for l in ls]
                    num = sum(w * o for w, o in zip(ws, os))
                    att_ref[0, rows, cols] = (num / sum(ws)).astype(att_ref.dtype)
                else:
                    o_scr[(store_g - 1) * n_pairs + p, rows, :] = o2
                    lse_scr[(store_g - 1) * n_pairs + p, rows, :] = lse2

        if one_class:
            stage_kv(0, 0)
        scores(0, 0)

        def body(it, carry, per_body=per_body, n_iters=n_iters, scores=scores, finish=finish):
            for j in range(per_body):
                if j + 1 < per_body:
                    scores(it, j + 1)
                else:
                    scores(jnp.minimum(it + 1, n_iters - 1), 0)
                finish(it, j)
            return carry

        lax.fori_loop(0, n_iters, body, 0)

        if two_hops:
            def second_hop(r2, carry, g=g, store_g=store_g):
                per_class = tm // MAX_STRIDE
                for scr in (o_scr, lse_scr):
                    for p in range(n_pairs):
                        for c in range(per_class // LANES):
                            src = pl.ds(pl.multiple_of(r2 * per_class, per_class) + c * LANES, LANES)
                            dst = pl.ds(r2 + MAX_STRIDE * c * LANES, LANES, stride=MAX_STRIDE)
                            scr[(g - 1) * n_pairs + p, dst, :] = scr[(store_g - 1) * n_pairs + p, src, :]
                return carry

            lax.fori_loop(0, MAX_STRIDE, second_hop, 0)


def _attn(qkv):
    nb = qkv[0].shape[0]
    seq = qkv[0].shape[1] * qkv[0].shape[2]
    tm = ATT_TILE
    operands = []
    in_specs = []
    kv_rows = 0
    for g, dil in enumerate(DILS):
        q, k, v = qkv[3 * g:3 * g + 3]
        n = tm // dil
        nblk = n // BAND
        n_halo = seq // dil // BAND
        cur = pl.BlockSpec((1, dil, n, GROUP_WIDTH), lambda b, i: (b, 0, i, 0))
        prev = pl.BlockSpec((1, dil, BAND, GROUP_WIDTH),
                            lambda b, i, nblk=nblk: (b, 0, jnp.maximum(i * nblk - 1, 0), 0))
        nxt = pl.BlockSpec((1, dil, BAND, GROUP_WIDTH),
                           lambda b, i, nblk=nblk, n_halo=n_halo: (b, 0, jnp.minimum((i + 1) * nblk, n_halo - 1), 0))
        operands += [q, k, k, k, v, v, v]
        in_specs += [cur, prev, cur, nxt, prev, cur, nxt]
        kv_rows = max(kv_rows, n - min(ATT_Q_BLOCK, n) + ATT_K_BLOCK)
    merge_shape = ((N_GROUPS - 1) * GROUP_WIDTH // LANES, tm, LANES)
    tq_small = min(ATT_Q_BLOCK, tm // max(DILS))
    scratch = [
        pltpu.VMEM(merge_shape, F32),
        pltpu.VMEM(merge_shape, F32),
        pltpu.VMEM((2, kv_rows, GROUP_WIDTH), BF16),
        pltpu.VMEM((2, kv_rows, GROUP_WIDTH), BF16),
        pltpu.VMEM((2, GROUP_WIDTH // LANES, ATT_K_BLOCK, 2 * ATT_Q_BLOCK), F32),
        pltpu.VMEM((2, GROUP_WIDTH // LANES, SUBLANES, 2 * ATT_Q_BLOCK), F32),
        pltpu.VMEM((4, ATT_K_BLOCK, ATT_Q_BLOCK), F32),
        pltpu.VMEM((2, GROUP_WIDTH // LANES, 2 * tq_small, ATT_K_BLOCK), F32),
        pltpu.VMEM((2, GROUP_WIDTH // LANES, 2 * tq_small, LANES), F32),
        pltpu.VMEM((4, tq_small, ATT_K_BLOCK), F32),
    ]
    return pl.pallas_call(
        _attn_kernel,
        grid=(nb, seq // tm),
        in_specs=in_specs,
        out_specs=pl.BlockSpec((1, tm, GROUP_WIDTH), lambda b, i: (b, i, 0)),
        out_shape=jax.ShapeDtypeStruct((nb, seq, GROUP_WIDTH), BF16),
        scratch_shapes=scratch,
        compiler_params=_params(2, "arbitrary"),
        name="attn",
    )(*operands)


def _mix_kernel(x_ref, att_ref, u_ref, up_ref, un_ref, mod_ref, g1_ref, cw_ref, cb_ref,
                w3_ref, wa_ref, wc_ref, wm_ref, o_ref):
    tm = x_ref.shape[1]
    i = pl.program_id(1)
    last = pl.num_programs(1) - 1
    mod = mod_ref[0]
    halo = up_ref.shape[1]
    cw = cw_ref[...]
    rp = PART_ROWS
    n_parts = tm // rp
    row_id = lax.broadcasted_iota(jnp.int32, (rp, 1), 0)

    for part in range(n_parts):
        rows = slice(part * rp, (part + 1) * rp)
        x = x_ref[0, rows]
        xb = _modulated_norm(x, g1_ref[...], mod[1:2], mod[0:1]).astype(BF16)

        u = u_ref[0, rows]
        if part == 0:
            u_before = jnp.where(i > 0, up_ref[0, halo - 1:halo, :], 0.0)
        else:
            u_before = u_ref[0, part * rp - 1:part * rp, :]
        if part == n_parts - 1:
            u_after = jnp.where(i < last, un_ref[0, 0:1, :], 0.0)
        else:
            u_after = u_ref[0, (part + 1) * rp:(part + 1) * rp + 1, :]
        u_m1 = jnp.where(row_id == 0, u_before, pltpu.roll(u, 1, 0))
        u_p1 = jnp.where(row_id == rp - 1, u_after, pltpu.roll(u, rp - 1, 0))
        conv = cb_ref[...] + u_m1 * cw[0:1] + u * cw[1:2] + u_p1 * cw[2:3]

        b_gate = jnp.dot(xb, w3_ref[:, 0:D_MODEL], preferred_element_type=F32)
        cv = jnp.dot((b_gate * conv).astype(BF16), wc_ref[...], preferred_element_type=F32)
        g_att = jnp.dot(xb, w3_ref[:, D_MODEL:2 * D_MODEL], preferred_element_type=F32)
        ap = jnp.dot(att_ref[0, rows], wa_ref[...], preferred_element_type=F32)
        merged = jax.nn.sigmoid(g_att) * ap
        g_conv = jnp.dot(xb, w3_ref[:, 2 * D_MODEL:3 * D_MODEL], preferred_element_type=F32)
        merged = merged + jax.nn.sigmoid(g_conv) * cv
        mix = jnp.dot(merged.astype(BF16), wm_ref[...], preferred_element_type=F32)
        o_ref[0, rows] = x + mod[2:3] * mix


def _mix(x, att, u, mod, g1, conv_w, conv_b, w_bgg, w_att, w_conv, w_mixo):
    nb, seq, _ = x.shape
    tm = MLP_TILE
    halo = 8
    nh = tm // halo
    return pl.pallas_call(
        _mix_kernel,
        grid=(nb, seq // tm),
        in_specs=[
            pl.BlockSpec((1, tm, D_MODEL), lambda b, i: (b, i, 0)),
            pl.BlockSpec((1, tm, GROUP_WIDTH), lambda b, i: (b, i, 0)),
            pl.BlockSpec((1, tm, D_MODEL), lambda b, i: (b, i, 0)),
            pl.BlockSpec((1, halo, D_MODEL), lambda b, i: (b, jnp.maximum(i * nh - 1, 0), 0)),
            pl.BlockSpec((1, halo, D_MODEL), lambda b, i: (b, jnp.minimum((i + 1) * nh, seq // halo - 1), 0)),
            pl.BlockSpec((1, 6, D_MODEL), lambda b, i: (b, 0, 0)),
            pl.BlockSpec((1, D_MODEL), lambda b, i: (0, 0)),
            pl.BlockSpec((CONV_WIDTH, D_MODEL), lambda b, i: (0, 0)),
            pl.BlockSpec((1, D_MODEL), lambda b, i: (0, 0)),
            _resident(w_bgg.shape),
            _resident(w_att.shape),
            _resident(w_conv.shape),
            _resident(w_mixo.shape),
        ],
        out_specs=pl.BlockSpec((1, tm, D_MODEL), lambda b, i: (b, i, 0)),
        out_shape=jax.ShapeDtypeStruct((nb, seq, D_MODEL), F32),
        compiler_params=_params(2),
        name="mix",
    )(x, att, u, u, u, mod, g1, conv_w, conv_b, w_bgg, w_att, w_conv, w_mixo)


def _mlp_kernel(x_ref, mod_ref, g2_ref, gf_ref, w1_ref, w2_ref, o_ref):
    mod = mod_ref[0]
    tm = x_ref.shape[1]
    rp = PART_ROWS
    ff_chunk = D_MODEL
    for part in range(tm // rp):
        rows = slice(part * rp, (part + 1) * rp)
        x = x_ref[0, rows]
        xb = _modulated_norm(x, g2_ref[...], mod[4:5], mod[3:4]).astype(BF16)
        acc = jnp.zeros(x.shape, F32)
        for c in range(D_FF // ff_chunk):
            cols = slice(c * ff_chunk, (c + 1) * ff_chunk)
            h = jnp.dot(xb, w1_ref[:, cols], preferred_element_type=F32)
            h = jnp.square(jnp.maximum(h, 0.0)).astype(BF16)
            acc = acc + jnp.dot(h, w2_ref[cols, :], preferred_element_type=F32)
        x2 = x + mod[5:6] * acc
        r = lax.rsqrt(jnp.mean(x2 * x2, axis=-1, keepdims=True) + EPS)
        o_ref[0, rows] = (x2 * r) * gf_ref[...]


def _mlp(x1, mod, g2, gf, w1, w2):
    nb, seq, _ = x1.shape
    tm = MLP_TILE
    return pl.pallas_call(
        _mlp_kernel,
        grid=(nb, seq // tm),
        in_specs=[
            pl.BlockSpec((1, tm, D_MODEL), lambda b, i: (b, i, 0)),
            pl.BlockSpec((1, 6, D_MODEL), lambda b, i: (b, 0, 0)),
            pl.BlockSpec((1, D_MODEL), lambda b, i: (0, 0)),
            pl.BlockSpec((1, D_MODEL), lambda b, i: (0, 0)),
            _resident(w1.shape),
            _resident(w2.shape),
        ],
        out_specs=pl.BlockSpec((1, tm, D_MODEL), lambda b, i: (b, i, 0)),
        out_shape=jax.ShapeDtypeStruct((nb, seq, D_MODEL), F32),
        compiler_params=_params(2),
        name="mlp",
    )(x1, mod, g2, gf, w1, w2)


def _rope_tables(seq, tm):
    inv = 1.0 / (ROPE_THETA ** (np.arange(HALF_ROT, dtype=np.float64) / HALF_ROT))
    inv = np.tile(inv, LANES // HALF_ROT)[None, :]
    sign = np.where((np.arange(LANES) // HALF_ROT) % 2 == 0, -1.0, 1.0)[None, :]
    offsets = []
    for dil in DILS:
        o = np.arange(tm).reshape(tm // PART_ROWS, PART_ROWS // dil, dil)
        offsets.append(o.transpose(0, 2, 1).reshape(tm))
    ang_b = np.stack(offsets).astype(np.float64)[:, :, None] * inv[None]
    ang_a = (np.arange(seq // tm, dtype=np.float64) * tm)[:, None] * inv
    base = np.stack([np.cos(ang_b), np.sin(ang_b), sign * np.sin(ang_b)], axis=1)
    tile = np.stack([np.cos(ang_a), np.sin(ang_a), sign * np.sin(ang_a)], axis=1)
    return jnp.asarray(base, F32), jnp.asarray(tile, F32)


def _encoder(x, mod, g1, g2, gf, conv_w, conv_b, w_in, w_bgg, w_att, w_conv, w_mixo, w1, w2):
    outs = _qkv(x, mod, g1, w_in)
    att = _attn(outs[:3 * N_GROUPS])
    u = outs[3 * N_GROUPS]
    x1 = _mix(x, att, u, mod, g1, conv_w, conv_b, w_bgg, w_att, w_conv, w_mixo)
    return _mlp(x1, mod, g2, gf, w1, w2)


def kernel(x_prompt, x_sample, c_prompt, c_sample, w_ada, b_ada, norm1_g, w_in, conv_w, conv_b,
           w_attn_out, w_conv_out, w_mix_out, norm2_g, w_mlp_in, w_mlp_out, final_norm_g):
    assert w_ada.shape[0] == 1, "single layer"
    nb_p, nb_s = x_prompt.shape[0], x_sample.shape[0]
    pad = -(nb_p + nb_s) % 8
    c_all = jnp.concatenate([c_prompt, c_sample, jnp.zeros((pad, D_MODEL), F32)], axis=0)
    mod = _ada(c_all, w_ada[0], b_ada[0][None, :])
    mod = mod.reshape(mod.shape[0], 6, D_MODEL)

    w_in_b = w_in[0].astype(BF16)
    w_bgg = w_in_b[:, 3 * ATT_WIDTH + 2 * D_MODEL:]
    shared = (norm1_g, norm2_g, final_norm_g[None, :], conv_w[0], conv_b, w_in_b, w_bgg,
              w_attn_out[0].astype(BF16), w_conv_out[0].astype(BF16), w_mix_out[0].astype(BF16),
              w_mlp_in[0].astype(BF16), w_mlp_out[0].astype(BF16))

    ys = []
    for x, m in ((x_prompt, mod[:nb_p]), (x_sample, mod[nb_p:nb_p + nb_s])):
        ys.append(_encoder(x, m, *shared))
    return tuple(ys)
```

```python
import functools

import jax
import jax.numpy as jnp
import numpy as np
from jax import lax
from jax.experimental import pallas as pl
from jax.experimental.pallas import tpu as pltpu

D_MODEL = 1024
HEAD_DIM = 64
HEADS_PER_GROUP = 8
GROUP_WIDTH = HEADS_PER_GROUP * HEAD_DIM
DILATION_GROUPS = ((128, 1), (512, 4), (2048, 16))
DILS = tuple(d for _, d in DILATION_GROUPS)
BAND = DILATION_GROUPS[0][0] // (2 * DILATION_GROUPS[0][1])
assert all(w // (2 * d) == BAND for w, d in DILATION_GROUPS)
N_GROUPS = len(DILS)
ATT_WIDTH = N_GROUPS * GROUP_WIDTH
CONV_WIDTH = 3
D_FF = 4 * D_MODEL
ROPE_THETA = 10000.0
EPS = 1e-6
NEG = -1e30

LANES = 128
SUBLANES = 8
HALF_ROT = HEAD_DIM // 2
QK_SCALE = HEAD_DIM ** -0.5 * np.log2(np.e)

GATE_COL = 3 * ATT_WIDTH + 2 * D_MODEL
GATE_BLOCK = GATE_COL // 2
assert GATE_BLOCK % LANES == 0 and GATE_BLOCK >= 3 * D_MODEL

QKV_TILE = 1024
MLP_TILE = 1024
PART_ROWS = 256
ATT_TILE = 1024
ATT_Q_BLOCK = 2 * BAND
ATT_K_BLOCK = 4 * BAND
MAX_STRIDE = 4
ATT_BLOCKS_PER_BODY = (8, 4, 16)
VMEM_LIMIT = 60 * 1024 * 1024

F32 = jnp.float32
BF16 = jnp.bfloat16


def _resident(shape):
    return pl.BlockSpec(shape, lambda *_: (0,) * len(shape), pipeline_mode=pl.Buffered(1))


def _params(n_axes, semantics="parallel"):
    return pltpu.CompilerParams(
        dimension_semantics=(semantics,) * n_axes, vmem_limit_bytes=VMEM_LIMIT)


def _modulated_norm(x, gain, scale, shift):
    r = lax.rsqrt(jnp.mean(x * x, axis=-1, keepdims=True) + EPS)
    return (x * r) * (gain * (1.0 + scale)) + shift


def _ada_kernel(c_ref, w_ref, b_ref, o_ref):
    c = c_ref[...]
    s = c * jax.nn.sigmoid(c)
    o_ref[...] = jnp.dot(s, w_ref[...], precision=lax.Precision.HIGHEST,
                         preferred_element_type=F32) + b_ref[...]


def _ada(c_all, w_ada, b_ada):
    rows, _ = c_all.shape
    n_out = w_ada.shape[1]
    bn = n_out // 4
    return pl.pallas_call(
        _ada_kernel,
        grid=(n_out // bn,),
        in_specs=[
            pl.BlockSpec((rows, D_MODEL), lambda j: (0, 0)),
            pl.BlockSpec((D_MODEL, bn), lambda j: (0, j)),
            pl.BlockSpec((1, bn), lambda j: (0, j)),
        ],
        out_specs=pl.BlockSpec((rows, bn), lambda j: (0, j)),
        out_shape=jax.ShapeDtypeStruct((rows, n_out), F32),
        compiler_params=_params(1),
        name="ada",
    )(c_all, w_ada, b_ada)


def _qkv_kernel(x_ref, mod_ref, g1_ref, rope_base_ref, rope_tile_ref, w_ref, *rest):
    qkv_refs = rest[:3 * N_GROUPS]
    u_ref = rest[3 * N_GROUPS]
    scr_ref = rest[3 * N_GROUPS + 1]
    tm = x_ref.shape[1]
    mod = mod_ref[0]
    ta = rope_tile_ref[0]
    ca, sa, ssa = ta[0:1], ta[1:2], ta[2:3]
    lane = lax.broadcasted_iota(jnp.int32, (1, LANES), 1)
    first_half = (lane // HALF_ROT) % 2 == 0

    rp = PART_ROWS
    for part in range(tm // rp):
        rows = slice(part * rp, (part + 1) * rp)
        xn = _modulated_norm(x_ref[0, rows], g1_ref[...], mod[1:2], mod[0:1])
        xb = xn.astype(BF16)

        lhs = {1: xb}
        prev_dil, prev = 1, xn
        for k, dil in enumerate(DILS[1:]):
            step = dil // prev_dil
            assert dil % prev_dil == 0 and step <= MAX_STRIDE
            n = rp // dil
            for c in range(D_MODEL // LANES):
                scr_ref[part % 2, k, c] = prev[:, c * LANES:(c + 1) * LANES]
            prev = jnp.concatenate(
                [jnp.concatenate([scr_ref[part % 2, k, c,
                                          pl.ds((r % prev_dil) * (rp // prev_dil) + r // prev_dil, n, stride=step), :]
                                  for r in range(dil)], axis=0)
                 for c in range(D_MODEL // LANES)], axis=1)
            lhs[dil] = prev.astype(BF16)
            prev_dil = dil

        def emit(out_ref, dil, val, part=part, rp=rp):
            n = rp // dil
            for r in range(dil):
                out_ref[0, r, part * n:(part + 1) * n] = val[r * n:(r + 1) * n].astype(BF16)

        def rope(val, scale, g, rows=rows):
            cb, sb, ssb = rope_base_ref[g, 0, rows], rope_base_ref[g, 1, rows], rope_base_ref[g, 2, rows]
            cos = (ca * cb - sa * sb) * scale
            sin = (ssa * cb + ca * ssb) * scale
            parts = []
            for p in range(GROUP_WIDTH // LANES):
                xp = val[:, p * LANES:(p + 1) * LANES]
                partner = jnp.where(first_half, pltpu.roll(xp, LANES - HALF_ROT, 1),
                                    pltpu.roll(xp, HALF_ROT, 1))
                parts.append(xp * cos + partner * sin)
            return jnp.concatenate(parts, axis=1)

        for which in range(3):
            for g, dil in enumerate(DILS):
                col = which * ATT_WIDTH + g * GROUP_WIDTH
                val = jnp.dot(lhs[dil], w_ref[:, col:col + GROUP_WIDTH], preferred_element_type=F32)
                if which < 2:
                    val = rope(val, QK_SCALE if which == 0 else 1.0, g)
                emit(qkv_refs[3 * g + which], dil, val)

        col = 3 * ATT_WIDTH
        h = jnp.dot(xb, w_ref[:, col:col + D_MODEL], preferred_element_type=F32)
        cg = jnp.dot(xb, w_ref[:, col + D_MODEL:col + 2 * D_MODEL], preferred_element_type=F32)
        u_ref[0, rows] = cg * h


def _qkv(x, mod, g1, w_in):
    nb, seq, _ = x.shape
    tm = QKV_TILE
    n_cols = 3 * ATT_WIDTH + 2 * D_MODEL
    rope_base, rope_tile = _rope_tables(seq, tm)
    out_shapes = []
    out_specs = []
    for dil in DILS:
        for _ in range(3):
            out_shapes.append(jax.ShapeDtypeStruct((nb, dil, seq // dil, GROUP_WIDTH), BF16))
            out_specs.append(pl.BlockSpec((1, dil, tm // dil, GROUP_WIDTH), lambda b, i: (b, 0, i, 0)))
    out_shapes.append(jax.ShapeDtypeStruct((nb, seq, D_MODEL), F32))
    out_specs.append(pl.BlockSpec((1, tm, D_MODEL), lambda b, i: (b, i, 0)))
    return pl.pallas_call(
        _qkv_kernel,
        grid=(nb, seq // tm),
        in_specs=[
            pl.BlockSpec((1, tm, D_MODEL), lambda b, i: (b, i, 0)),
            pl.BlockSpec((1, 6, D_MODEL), lambda b, i: (b, 0, 0)),
            pl.BlockSpec((1, D_MODEL), lambda b, i: (0, 0)),
            _resident((N_GROUPS, 3, tm, LANES)),
            pl.BlockSpec((1, 3, LANES), lambda b, i: (i, 0, 0)),
            pl.BlockSpec((D_MODEL, n_cols), lambda b, i: (0, 0), pipeline_mode=pl.Buffered(1)),
        ],
        out_specs=out_specs,
        out_shape=out_shapes,
        scratch_shapes=[pltpu.VMEM((2, N_GROUPS - 1, D_MODEL // LANES, PART_ROWS, LANES), F32)],
        compiler_params=_params(2),
        name="qkv",
    )(x, mod, g1, rope_base, rope_tile, w_in)


def _attn_kernel(*refs):
    in_refs = refs[:7 * N_GROUPS]
    att_ref = refs[7 * N_GROUPS]
    (o_scr, lse_scr, kf_scr, vf_scr, st_scr, mt_scr, biast_scr,
     sr_scr, mr_scr, biasr_scr) = refs[7 * N_GROUPS + 1:]
    tm = att_ref.shape[1]
    i = pl.program_id(1)
    last = pl.num_programs(1) - 1

    n_pairs = GROUP_WIDTH // LANES
    lane = lax.broadcasted_iota(jnp.int32, (1, LANES), 1)
    head_a = lane < HEAD_DIM
    tk = ATT_K_BLOCK
    mask_forms = {}

    def build_masks():
        for (transposed, tq), bias_scr in mask_forms.items():
            if transposed:
                key = lax.broadcasted_iota(jnp.int32, (tk, tq), 0)
                qry = lax.broadcasted_iota(jnp.int32, (tk, tq), 1)
            else:
                qry = lax.broadcasted_iota(jnp.int32, (tq, tk), 0)
                key = lax.broadcasted_iota(jnp.int32, (tq, tk), 1)
            rel = key - BAND - qry
            band = (rel >= -BAND) & (rel <= BAND)
            after_start = key >= BAND
            before_end = key < tq + BAND
            for variant, valid in enumerate((band, band & after_start, band & before_end,
                                             band & after_start & before_end)):
                bias_scr[variant] = jnp.where(valid, 0.0, NEG).astype(F32)

    for g, dil in enumerate(DILS):
        tq = min(ATT_Q_BLOCK, tm // dil)
        mask_forms.setdefault((tq == LANES, tq), biast_scr if tq == LANES else biasr_scr)
    pl.when(jnp.logical_and(pl.program_id(0) == 0, i == 0))(build_masks)

    assert DILS[0] == 1 and min(DILS[1:]) > 1
    pending_hop = None
    for g in sorted(range(N_GROUPS), key=lambda g: -DILS[g]):
        dil = DILS[g]
        q_ref, kp_ref, kc_ref, kn_ref, vp_ref, vc_ref, vn_ref = in_refs[7 * g:7 * g + 7]
        two_hops = dil > MAX_STRIDE
        if two_hops:
            assert dil % MAX_STRIDE == 0 and dil // MAX_STRIDE <= MAX_STRIDE and MAX_STRIDE in DILS
        store_g = DILS.index(MAX_STRIDE) if two_hops else g
        n = tm // dil
        tq = min(ATT_Q_BLOCK, n)
        nsub = n // tq
        per_body = ATT_BLOCKS_PER_BODY[g]
        n_iters = dil * nsub // per_body
        one_class = nsub >= per_body
        if one_class:
            assert dil == 1 and nsub % per_body == 0
        else:
            assert per_body % (2 * nsub) == 0 and dil % (per_body // nsub) == 0

        transposed = tq == LANES
        s_scr, m_scr, bias_scr = (st_scr, mt_scr, biast_scr) if transposed else (sr_scr, mr_scr, biasr_scr)

        def block_coords(it, j, nsub=nsub, per_body=per_body, one_class=one_class):
            if one_class:
                return 0, it * per_body + j, 0
            return it * (per_body // nsub) + j // nsub, j % nsub, (j // nsub) % 2

        def stage_kv(r, slot, n=n, tq=tq, kp_ref=kp_ref, kc_ref=kc_ref, kn_ref=kn_ref,
                     vp_ref=vp_ref, vc_ref=vc_ref, vn_ref=vn_ref):
            pad = tk - tq - 2 * BAND
            for full, before, cur, after in ((kf_scr, kp_ref, kc_ref, kn_ref),
                                             (vf_scr, vp_ref, vc_ref, vn_ref)):
                full[slot, 0:BAND] = before[0, r]
                full[slot, BAND:BAND + n] = cur[0, r]
                full[slot, BAND + n:2 * BAND + n] = after[0, r]
                if pad:
                    full[slot, 2 * BAND + n:2 * BAND + n + pad] = jnp.zeros((pad, GROUP_WIDTH), full.dtype)

        def edge_bias(sub, nsub=nsub, bias_scr=bias_scr):
            first = jnp.logical_and(i == 0, sub == 0)
            final = jnp.logical_and(i == last, sub == nsub - 1)
            return bias_scr[first.astype(jnp.int32) + 2 * final.astype(jnp.int32)]

        def scores(it, j, tq=tq, q_ref=q_ref, block_coords=block_coords, stage_kv=stage_kv,
                   one_class=one_class, edge_bias=edge_bias, transposed=transposed,
                   s_scr=s_scr, m_scr=m_scr):
            r, sub, kv_slot = block_coords(it, j)
            if not one_class and sub == 0:
                stage_kv(r, kv_slot)
            q0 = sub * tq if isinstance(sub, int) else pl.multiple_of(sub * tq, tq)
            bias = edge_bias(sub)
            for p in range(n_pairs):
                cols = slice(p * LANES, (p + 1) * LANES)
                qp = q_ref[0, r, pl.ds(q0, tq), cols]
                kp = kf_scr[kv_slot, pl.ds(q0, tk), cols]
                zeros = jnp.zeros_like(qp)
                q_heads = (jnp.where(head_a, qp, zeros), jnp.where(head_a, zeros, qp))
                if transposed:
                    s = lax.dot_general(kp, jnp.concatenate(q_heads, axis=0),
                                        (((1,), (1,)), ((), ())), preferred_element_type=F32)
                    s = jnp.concatenate([s[:, 0:tq] + bias, s[:, tq:2 * tq] + bias], axis=1)
                    s_scr[j % 2, p] = s
                    m_scr[j % 2, p] = jnp.broadcast_to(jnp.max(s, axis=0, keepdims=True),
                                                       (SUBLANES, 2 * tq))
                else:
                    s = lax.dot_general(jnp.concatenate(q_heads, axis=0), kp, (((1,), (1,)), ((), ())),
                                        preferred_element_type=F32)
                    s = s + jnp.concatenate([bias, bias], axis=0)
                    s_scr[j % 2, p] = s
                    m_scr[j % 2, p] = jnp.broadcast_to(jnp.max(s, axis=-1, keepdims=True),
                                                       (2 * tq, LANES))

        def finish(it, j, tq=tq, dil=dil, g=g, block_coords=block_coords, transposed=transposed,
                   s_scr=s_scr, m_scr=m_scr):
            r, sub, kv_slot = block_coords(it, j)
            q0 = sub * tq if isinstance(sub, int) else pl.multiple_of(sub * tq, tq)
            for p in range(n_pairs):
                cols = slice(p * LANES, (p + 1) * LANES)
                vp = vf_scr[kv_slot, pl.ds(q0, tk), cols]
                if transposed:
                    vt = vp.T
                    ones = jnp.ones((2 * SUBLANES, tk), BF16)
                    m = m_scr[j % 2, p]
                    e = jnp.exp2(s_scr[j % 2, p] - m[0:1]).astype(BF16)
                    tops, lses = [], []
                    for hh in range(2):
                        lhs = jnp.concatenate([vt[hh * HEAD_DIM:(hh + 1) * HEAD_DIM], ones], axis=0)
                        res = jnp.dot(lhs, e[:, hh * tq:(hh + 1) * tq], preferred_element_type=F32)
                        den = res[HEAD_DIM:HEAD_DIM + SUBLANES]
                        tops.append(res[0:HEAD_DIM] * jnp.tile(1.0 / den, (HEAD_DIM // SUBLANES, 1)))
                        lse = m[:, hh * tq:(hh + 1) * tq] + jnp.log2(den)
                        lses.append(jnp.tile(lse, (HEAD_DIM // SUBLANES, 1)))
                    o2 = jnp.concatenate(tops, axis=0).T
                    lse2 = jnp.concatenate(lses, axis=0).T
                else:
                    s = s_scr[j % 2, p]
                    m = m_scr[j % 2, p]
                    e = jnp.concatenate([jnp.exp2(s[:, t * LANES:(t + 1) * LANES] - m)
                                         for t in range(tk // LANES)], axis=1)
                    rhs = jnp.concatenate([vp, jnp.ones_like(vp)], axis=1)
                    res = jnp.dot(e.astype(BF16), rhs, preferred_element_type=F32)
                    den = jnp.where(head_a, res[0:tq, LANES:2 * LANES], res[tq:2 * tq, LANES:2 * LANES])
                    o2 = jnp.where(head_a, res[0:tq, 0:LANES], res[tq:2 * tq, 0:LANES]) / den
                    lse2 = jnp.where(head_a, m[0:tq], m[tq:2 * tq]) + jnp.log2(den)
                if dil == 1:
                    rows = pl.ds(q0, tq)
                elif two_hops:
                    inner = dil // MAX_STRIDE
                    rows = pl.ds((r % MAX_STRIDE) * (tm // MAX_STRIDE) + r // MAX_STRIDE + inner * q0,
                                 tq, stride=inner)
                else:
                    rows = pl.ds(r + dil * q0, tq, stride=dil)
                if dil == 1:
                    ls = [lse2] + [lse_scr[s * n_pairs + p, rows, :] for s in range(N_GROUPS - 1)]
                    os = [o2] + [o_scr[s * n_pairs + p, rows, :] for s in range(N_GROUPS - 1)]
                    mx = functools.reduce(jnp.maximum, ls)
                    ws = [jnp.exp2(l - mx) for l in ls]
                    num = sum(w * o for w, o in zip(ws, os))
                    att_ref[0, rows, cols] = (num / sum(ws)).astype(att_ref.dtype)
                else:
                    o_scr[(store_g - 1) * n_pairs + p, rows, :] = o2
                    lse_scr[(store_g - 1) * n_pairs + p, rows, :] = lse2

        if one_class:
            stage_kv(0, 0)
        scores(0, 0)
        if pending_hop is not None:
            pending_hop()
            pending_hop = None

        def body(it, carry, per_body=per_body, n_iters=n_iters, scores=scores, finish=finish):
            for j in range(per_body):
                if j + 1 < per_body:
                    scores(it, j + 1)
                else:
                    scores(jnp.minimum(it + 1, n_iters - 1), 0)
                finish(it, j)
            return carry

        lax.fori_loop(0, n_iters, body, 0)

        if two_hops:
            def second_hop(g=g, store_g=store_g):
                per_class = tm // MAX_STRIDE
                for r2 in range(MAX_STRIDE):
                    for scr in (o_scr, lse_scr):
                        for p in range(n_pairs):
                            for c in range(per_class // LANES):
                                src = pl.ds(r2 * per_class + c * LANES, LANES)
                                dst = pl.ds(r2 + MAX_STRIDE * c * LANES, LANES, stride=MAX_STRIDE)
                                scr[(g - 1) * n_pairs + p, dst, :] = scr[(store_g - 1) * n_pairs + p, src, :]

            pending_hop = second_hop
    assert pending_hop is None


def _attn(qkv):
    nb = qkv[0].shape[0]
    seq = qkv[0].shape[1] * qkv[0].shape[2]
    tm = ATT_TILE
    operands = []
    in_specs = []
    kv_rows = 0
    for g, dil in enumerate(DILS):
        q, k, v = qkv[3 * g:3 * g + 3]
        n = tm // dil
        nblk = n // BAND
        n_halo = seq // dil // BAND
        cur = pl.BlockSpec((1, dil, n, GROUP_WIDTH), lambda b, i: (b, 0, i, 0))
        prev = pl.BlockSpec((1, dil, BAND, GROUP_WIDTH),
                            lambda b, i, nblk=nblk: (b, 0, jnp.maximum(i * nblk - 1, 0), 0))
        nxt = pl.BlockSpec((1, dil, BAND, GROUP_WIDTH),
                           lambda b, i, nblk=nblk, n_halo=n_halo: (b, 0, jnp.minimum((i + 1) * nblk, n_halo - 1), 0))
        operands += [q, k, k, k, v, v, v]
        in_specs += [cur, prev, cur, nxt, prev, cur, nxt]
        kv_rows = max(kv_rows, n - min(ATT_Q_BLOCK, n) + ATT_K_BLOCK)
    merge_shape = ((N_GROUPS - 1) * GROUP_WIDTH // LANES, tm, LANES)
    tq_small = min(ATT_Q_BLOCK, tm // max(DILS))
    scratch = [
        pltpu.VMEM(merge_shape, F32),
        pltpu.VMEM(merge_shape, F32),
        pltpu.VMEM((2, kv_rows, GROUP_WIDTH), BF16),
        pltpu.VMEM((2, kv_rows, GROUP_WIDTH), BF16),
        pltpu.VMEM((2, GROUP_WIDTH // LANES, ATT_K_BLOCK, 2 * ATT_Q_BLOCK), F32),
        pltpu.VMEM((2, GROUP_WIDTH // LANES, SUBLANES, 2 * ATT_Q_BLOCK), F32),
        pltpu.VMEM((4, ATT_K_BLOCK, ATT_Q_BLOCK), F32),
        pltpu.VMEM((2, GROUP_WIDTH // LANES, 2 * tq_small, ATT_K_BLOCK), F32),
        pltpu.VMEM((2, GROUP_WIDTH // LANES, 2 * tq_small, LANES), F32),
        pltpu.VMEM((4, tq_small, ATT_K_BLOCK), F32),
    ]
    return pl.pallas_call(
        _attn_kernel,
        grid=(nb, seq // tm),
        in_specs=in_specs,
        out_specs=pl.BlockSpec((1, tm, GROUP_WIDTH), lambda b, i: (b, i, 0)),
        out_shape=jax.ShapeDtypeStruct((nb, seq, GROUP_WIDTH), BF16),
        scratch_shapes=scratch,
        compiler_params=_params(2, "arbitrary"),
        name="attn",
    )(*operands)


def _mix_kernel(x_ref, att_ref, u_ref, up_ref, un_ref, mod_ref, g1_ref, cw_ref, cb_ref,
                w3_ref, wa_ref, wc_ref, wm_ref, o_ref):
    tm = x_ref.shape[1]
    i = pl.program_id(1)
    last = pl.num_programs(1) - 1
    mod = mod_ref[0]
    halo = up_ref.shape[1]
    cw = cw_ref[...]
    rp = PART_ROWS
    n_parts = tm // rp
    row_id = lax.broadcasted_iota(jnp.int32, (rp, 1), 0)

    for part in range(n_parts):
        rows = slice(part * rp, (part + 1) * rp)
        x = x_ref[0, rows]
        xb = _modulated_norm(x, g1_ref[...], mod[1:2], mod[0:1]).astype(BF16)

        u = u_ref[0, rows]
        if part == 0:
            u_before = jnp.where(i > 0, up_ref[0, halo - 1:halo, :], 0.0)
        else:
            u_before = u_ref[0, part * rp - 1:part * rp, :]
        if part == n_parts - 1:
            u_after = jnp.where(i < last, un_ref[0, 0:1, :], 0.0)
        else:
            u_after = u_ref[0, (part + 1) * rp:(part + 1) * rp + 1, :]
        u_m1 = jnp.where(row_id == 0, u_before, pltpu.roll(u, 1, 0))
        u_p1 = jnp.where(row_id == rp - 1, u_after, pltpu.roll(u, rp - 1, 0))
        conv = cb_ref[...] + u_m1 * cw[0:1] + u * cw[1:2] + u_p1 * cw[2:3]

        b_gate = jnp.dot(xb, w3_ref[:, 0:D_MODEL], preferred_element_type=F32)
        cv = jnp.dot((b_gate * conv).astype(BF16), wc_ref[...], preferred_element_type=F32)
        g_att = jnp.dot(xb, w3_ref[:, D_MODEL:2 * D_MODEL], preferred_element_type=F32)
        ap = jnp.dot(att_ref[0, rows], wa_ref[...], preferred_element_type=F32)
        merged = jax.nn.sigmoid(g_att) * ap
        g_conv = jnp.dot(xb, w3_ref[:, 2 * D_MODEL:3 * D_MODEL], preferred_element_type=F32)
        merged = merged + jax.nn.sigmoid(g_conv) * cv
        mix = jnp.dot(merged.astype(BF16), wm_ref[...], preferred_element_type=F32)
        o_ref[0, rows] = x + mod[2:3] * mix


def _mix(x, att, u, mod, g1, conv_w, conv_b, w_bgg, w_att, w_conv, w_mixo):
    nb, seq, _ = x.shape
    tm = MLP_TILE
    halo = 8
    nh = tm // halo
    return pl.pallas_call(
        _mix_kernel,
        grid=(nb, seq // tm),
        in_specs=[
            pl.BlockSpec((1, tm, D_MODEL), lambda b, i: (b, i, 0)),
            pl.BlockSpec((1, tm, GROUP_WIDTH), lambda b, i: (b, i, 0)),
            pl.BlockSpec((1, tm, D_MODEL), lambda b, i: (b, i, 0)),
            pl.BlockSpec((1, halo, D_MODEL), lambda b, i: (b, jnp.maximum(i * nh - 1, 0), 0)),
            pl.BlockSpec((1, halo, D_MODEL), lambda b, i: (b, jnp.minimum((i + 1) * nh, seq // halo - 1), 0)),
            pl.BlockSpec((1, 6, D_MODEL), lambda b, i: (b, 0, 0)),
            pl.BlockSpec((1, D_MODEL), lambda b, i: (0, 0)),
            pl.BlockSpec((CONV_WIDTH, D_MODEL), lambda b, i: (0, 0)),
            pl.BlockSpec((1, D_MODEL), lambda b, i: (0, 0)),
            pl.BlockSpec((D_MODEL, GATE_BLOCK), lambda b, i: (0, GATE_COL // GATE_BLOCK),
                         pipeline_mode=pl.Buffered(1)),
            _resident(w_att.shape),
            _resident(w_conv.shape),
            _resident(w_mixo.shape),
        ],
        out_specs=pl.BlockSpec((1, tm, D_MODEL), lambda b, i: (b, i, 0)),
        out_shape=jax.ShapeDtypeStruct((nb, seq, D_MODEL), F32),
        compiler_params=_params(2),
        name="mix",
    )(x, att, u, u, u, mod, g1, conv_w, conv_b, w_bgg, w_att, w_conv, w_mixo)


def _mlp_kernel(x_ref, mod_ref, g2_ref, gf_ref, w1_ref, w2_ref, o_ref):
    mod = mod_ref[0]
    tm = x_ref.shape[1]
    rp = PART_ROWS
    ff_chunk = D_MODEL
    for part in range(tm // rp):
        rows = slice(part * rp, (part + 1) * rp)
        x = x_ref[0, rows]
        xb = _modulated_norm(x, g2_ref[...], mod[4:5], mod[3:4]).astype(BF16)
        acc = jnp.zeros(x.shape, F32)
        for c in range(D_FF // ff_chunk):
            cols = slice(c * ff_chunk, (c + 1) * ff_chunk)
            h = jnp.dot(xb, w1_ref[:, cols], preferred_element_type=F32)
            h = jnp.square(jnp.maximum(h, 0.0)).astype(BF16)
            acc = acc + jnp.dot(h, w2_ref[cols, :], preferred_element_type=F32)
        x2 = x + mod[5:6] * acc
        r = lax.rsqrt(jnp.mean(x2 * x2, axis=-1, keepdims=True) + EPS)
        o_ref[0, rows] = (x2 * r) * gf_ref[...]


def _mlp(x1, mod, g2, gf, w1, w2):
    nb, seq, _ = x1.shape
    tm = MLP_TILE
    return pl.pallas_call(
        _mlp_kernel,
        grid=(nb, seq // tm),
        in_specs=[
            pl.BlockSpec((1, tm, D_MODEL), lambda b, i: (b, i, 0)),
            pl.BlockSpec((1, 6, D_MODEL), lambda b, i: (b, 0, 0)),
            pl.BlockSpec((1, D_MODEL), lambda b, i: (0, 0)),
            pl.BlockSpec((1, D_MODEL), lambda b, i: (0, 0)),
            _resident(w1.shape),
            _resident(w2.shape),
        ],
        out_specs=pl.BlockSpec((1, tm, D_MODEL), lambda b, i: (b, i, 0)),
        out_shape=jax.ShapeDtypeStruct((nb, seq, D_MODEL), F32),
        compiler_params=_params(2),
        name="mlp",
    )(x1, mod, g2, gf, w1, w2)


def _rope_tables(seq, tm):
    inv = 1.0 / (ROPE_THETA ** (np.arange(HALF_ROT, dtype=np.float64) / HALF_ROT))
    inv = np.tile(inv, LANES // HALF_ROT)[None, :]
    sign = np.where((np.arange(LANES) // HALF_ROT) % 2 == 0, -1.0, 1.0)[None, :]
    offsets = []
    for dil in DILS:
        o = np.arange(tm).reshape(tm // PART_ROWS, PART_ROWS // dil, dil)
        offsets.append(o.transpose(0, 2, 1).reshape(tm))
    ang_b = np.stack(offsets).astype(np.float64)[:, :, None] * inv[None]
    ang_a = (np.arange(seq // tm, dtype=np.float64) * tm)[:, None] * inv
    base = np.stack([np.cos(ang_b), np.sin(ang_b), sign * np.sin(ang_b)], axis=1)
    tile = np.stack([np.cos(ang_a), np.sin(ang_a), sign * np.sin(ang_a)], axis=1)
    return jnp.asarray(base, F32), jnp.asarray(tile, F32)


def _encoder(x, mod, g1, g2, gf, conv_w, conv_b, w_in, w_bgg, w_att, w_conv, w_mixo, w1, w2):
    outs = _qkv(x, mod, g1, w_in)
    att = _attn(outs[:3 * N_GROUPS])
    u = outs[3 * N_GROUPS]
    x1 = _mix(x, att, u, mod, g1, conv_w, conv_b, w_bgg, w_att, w_conv, w_mixo)
    return _mlp(x1, mod, g2, gf, w1, w2)


def kernel(x_prompt, x_sample, c_prompt, c_sample, w_ada, b_ada, norm1_g, w_in, conv_w, conv_b,
           w_attn_out, w_conv_out, w_mix_out, norm2_g, w_mlp_in, w_mlp_out, final_norm_g):
    assert w_ada.shape[0] == 1, "single layer"
    nb_p, nb_s = x_prompt.shape[0], x_sample.shape[0]
    pad = -(nb_p + nb_s) % 8
    c_all = jnp.concatenate([c_prompt, c_sample, jnp.zeros((pad, D_MODEL), F32)], axis=0)
    mod = _ada(c_all, w_ada[0], b_ada[0][None, :])
    mod = mod.reshape(mod.shape[0], 6, D_MODEL)

    w_in_b = w_in[0].astype(BF16)
    shared = (norm1_g, norm2_g, final_norm_g[None, :], conv_w[0], conv_b, w_in_b, w_in_b,
              w_attn_out[0].astype(BF16), w_conv_out[0].astype(BF16), w_mix_out[0].astype(BF16),
              w_mlp_in[0].astype(BF16), w_mlp_out[0].astype(BF16))

    ys = []
    for x, m in ((x_prompt, mod[:nb_p]), (x_sample, mod[nb_p:nb_p + nb_s])):
        ys.append(_encoder(x, m, *shared))
    return tuple(ys)
```
